```python
import jax, jax.numpy as jnp
from jax import lax
import numpy as np

D_MODEL = 1024
BATCH = 16
SEQ = 2048
DEPTH = 2

N_MIXERS = 2
NORM_EPS = 1e-6
NEG = -1e30
D_FF = 4 * D_MODEL
RET_HEADS = 4
RET_DK = D_MODEL // RET_HEADS
RET_DV = 2 * RET_DK
RET_CHUNK = 128
RET_THETA = 10000.0
RET_IN = 2 * RET_HEADS * RET_DK + 2 * RET_HEADS * RET_DV
MOBA_HEADS = 8
MOBA_DH = D_MODEL // MOBA_HEADS
MOBA_BLOCK = 256
MOBA_TOPK = 3
MOBA_QCHUNK = 8
ROPE_THETA = 500000.0
ROPE_DIM = MOBA_DH // 4
MOBA_IN = 3 * MOBA_HEADS * MOBA_DH

kernel_name = "hybrid_retention_moba_block"


def rms_norm(x, g):
    xf = x.astype(jnp.float32)
    y = xf * lax.rsqrt(jnp.mean(xf * xf, axis=-1, keepdims=True) + NORM_EPS)
    return (y * g.astype(jnp.float32)).astype(x.dtype)


def sq_relu_mlp(x, w_up, w_down):
    h = jax.nn.relu(x @ w_up)
    return (h * h) @ w_down


def retnet_rotary(x, pos):
    d = x.shape[-1]
    freq = 1.0 / (RET_THETA ** jnp.linspace(0.0, 1.0, d // 2, dtype=jnp.float32))
    ang = pos[:, None].astype(jnp.float32) * freq[None, :]
    cos, sin = jnp.cos(ang), jnp.sin(ang)
    x1 = x[..., 0::2].astype(jnp.float32)
    x2 = x[..., 1::2].astype(jnp.float32)
    y = jnp.stack([x1 * cos - x2 * sin, x1 * sin + x2 * cos], axis=-1)
    return y.reshape(x.shape).astype(x.dtype)


def partial_rope(x, pos):
    half = ROPE_DIM // 2
    inv = ROPE_THETA ** (-jnp.arange(0, ROPE_DIM, 2, dtype=jnp.float32) / ROPE_DIM)
    ang = pos[:, None].astype(jnp.float32) * inv[None, :]
    cos, sin = jnp.cos(ang), jnp.sin(ang)
    x1 = x[..., :half].astype(jnp.float32)
    x2 = x[..., half:ROPE_DIM].astype(jnp.float32)
    rot = jnp.concatenate([x1 * cos - x2 * sin, x2 * cos + x1 * sin], axis=-1)
    return jnp.concatenate([rot.astype(x.dtype), x[..., ROPE_DIM:]], axis=-1)


def retention(h, w_in, w_out):
    B, S, _ = h.shape
    H, DK, DV, L = RET_HEADS, RET_DK, RET_DV, RET_CHUNK
    C = S // L
    q, k, v, g = jnp.split(h @ w_in, [H * DK, 2 * H * DK, 2 * H * DK + H * DV], axis=-1)
    q = q.reshape(B, S, H, DK).transpose(0, 2, 1, 3)
    k = k.reshape(B, S, H, DK).transpose(0, 2, 1, 3)
    v = v.reshape(B, S, H, DV).transpose(0, 2, 1, 3)
    pos = jnp.arange(S)
    q = retnet_rotary(q, pos)
    k = retnet_rotary(k, pos) * (DK ** -0.5)

    log_gamma = jnp.log(1.0 - 2.0 ** (-5.0 - jnp.arange(H, dtype=jnp.float32)))
    idx = jnp.arange(L, dtype=jnp.float32)
    diff = idx[:, None] - idx[None, :]
    d_intra = jnp.where(diff >= 0, jnp.exp(log_gamma[:, None, None] * jnp.maximum(diff, 0.0)), 0.0)
    q_decay = jnp.exp(log_gamma[:, None] * (idx + 1.0))[None, :, :, None]
    k_decay = jnp.exp(log_gamma[:, None] * (L - 1.0 - idx))[None, :, :, None]
    chunk_decay = jnp.exp(log_gamma * L)[None, :, None, None]

    qc = q.reshape(B, H, C, L, DK)
    kc = k.reshape(B, H, C, L, DK)
    vc = v.reshape(B, H, C, L, DV)
    scores = jnp.einsum('bhcid,bhcjd->bhcij', qc, kc).astype(jnp.float32) * d_intra[None, :, None]
    o_intra = jnp.einsum('bhcij,bhcje->bhcie', scores, vc)

    def step(state, inp):
        q_c, k_c, v_c = inp
        cross = jnp.einsum('bhid,bhde->bhie', q_c, state) * q_decay
        state = state * chunk_decay + jnp.einsum('bhjd,bhje->bhde', k_c * k_decay, v_c)
        return state, cross

    init = jnp.zeros((B, H, DK, DV), jnp.float32)
    xs = (qc.transpose(2, 0, 1, 3, 4), kc.transpose(2, 0, 1, 3, 4), vc.transpose(2, 0, 1, 3, 4))
    _, o_cross = lax.scan(step, init, xs)
    o = (o_intra + o_cross.transpose(1, 2, 0, 3, 4)).reshape(B, H, S, DV)
    o = o * lax.rsqrt(jnp.mean(o * o, axis=-1, keepdims=True) + NORM_EPS)
    o = o.transpose(0, 2, 1, 3).reshape(B, S, H * DV)
    o = o * jax.nn.silu(g.astype(jnp.float32))
    return o.astype(h.dtype) @ w_out


def moba_attention(h, w_in, w_out):
    B, S, _ = h.shape
    H, Dh, BS, QC = MOBA_HEADS, MOBA_DH, MOBA_BLOCK, MOBA_QCHUNK
    nb = -(-S // BS)
    Sp = nb * BS
    K_SEL = min(MOBA_TOPK, nb)
    q, k, v = jnp.split(h @ w_in, 3, axis=-1)
    q = q.reshape(B, S, H, Dh).transpose(0, 2, 1, 3)
    k = k.reshape(B, S, H, Dh).transpose(0, 2, 1, 3)
    v = v.reshape(B, S, H, Dh).transpose(0, 2, 1, 3)
    pos = jnp.arange(S)
    q = partial_rope(q, pos) * (Dh ** -0.5)
    k = partial_rope(k, pos)
    pad = ((0, 0), (0, 0), (0, Sp - S), (0, 0))
    q, k, v = jnp.pad(q, pad), jnp.pad(k, pad), jnp.pad(v, pad)
    qb = q.reshape(B, H, nb, BS, Dh)
    kb = k.reshape(B, H, nb, BS, Dh)
    vb = v.reshape(B, H, nb, BS, Dh)

    causal = jnp.tril(jnp.ones((BS, BS), dtype=bool))
    s_own = jnp.einsum('bhnid,bhnjd->bhnij', qb, kb).astype(jnp.float32)
    s_own = jnp.where(causal, s_own, NEG)
    m_own = jnp.max(s_own, axis=-1)
    p_own = jnp.exp(s_own - m_own[..., None])
    l_own = jnp.sum(p_own, axis=-1)
    o_own = jnp.einsum('bhnij,bhnjd->bhnid', p_own, vb)
    m_own, l_own = m_own.reshape(B, H, Sp), l_own.reshape(B, H, Sp)
    o_own = o_own.reshape(B, H, Sp, Dh)

    k_mean = jnp.mean(kb.astype(jnp.float32), axis=3)
    gate = jnp.einsum('bhsd,bhnd->bhsn', q.astype(jnp.float32), k_mean)
    q_blk = jnp.arange(Sp) // BS
    past = jnp.arange(nb)[None, :] < q_blk[:, None]
    gate = jnp.where(past, gate, NEG)
    _, sel = lax.top_k(gate, K_SEL)
    valid = sel < q_blk[:, None]

    nq = Sp // QC
    q_ch = q.reshape(B, H, nq, QC, Dh).transpose(2, 0, 1, 3, 4)
    sel_ch = sel.reshape(B, H, nq, QC, K_SEL).transpose(2, 0, 1, 3, 4)
    val_ch = valid.reshape(B, H, nq, QC, K_SEL).transpose(2, 0, 1, 3, 4)
    bi = jnp.arange(B)[:, None, None, None]
    hi = jnp.arange(H)[None, :, None, None]

    def attend_selected(args):
        qc, sc, vc = args
        k_sel = kb[bi, hi, sc]
        v_sel = vb[bi, hi, sc]
        s = jnp.einsum('bhqd,bhqkjd->bhqkj', qc, k_sel).astype(jnp.float32)
        s = jnp.where(vc[..., None], s, NEG)
        m = jnp.max(s, axis=(3, 4))
        p = jnp.exp(s - m[..., None, None])
        l = jnp.sum(p, axis=(3, 4))
        o = jnp.einsum('bhqkj,bhqkjd->bhqd', p, v_sel)
        return m, l, o

    m_sel, l_sel, o_sel = lax.map(attend_selected, (q_ch, sel_ch, val_ch))
    m_sel = m_sel.transpose(1, 2, 0, 3).reshape(B, H, Sp)
    l_sel = l_sel.transpose(1, 2, 0, 3).reshape(B, H, Sp)
    o_sel = o_sel.transpose(1, 2, 0, 3, 4).reshape(B, H, Sp, Dh)

    m = jnp.maximum(m_own, m_sel)
    a = jnp.exp(m_own - m)
    b = jnp.exp(m_sel - m)
    o = (o_own * a[..., None] + o_sel * b[..., None]) / (l_own * a + l_sel * b)[..., None]
    o = o[:, :, :S].transpose(0, 2, 1, 3).reshape(B, S, H * Dh)
    return o.astype(h.dtype) @ w_out


def setup_inputs(seed: int = 0) -> dict:
    key = jax.random.key(seed)
    ks = jax.random.split(key, 32)
    f32 = jnp.float32

    def w(k, shape):
        return jax.random.normal(k, shape, f32) * (shape[0] ** -0.5)

    def gain(k):
        return 1.0 + 0.05 * jax.random.normal(k, (D_MODEL,), f32)

    return {
        "x": jax.random.normal(ks[0], (BATCH, SEQ, D_MODEL), f32),
        "ln_mix_pre_0": gain(ks[1]),
        "ln_mix_post_0": gain(ks[2]),
        "w_in_0": w(ks[3], (D_MODEL, RET_IN)),
        "w_out_0": w(ks[4], (RET_HEADS * RET_DV, D_MODEL)),
        "ln_mlp_pre_0": gain(ks[5]),
        "ln_mlp_post_0": gain(ks[6]),
        "w_up_0": w(ks[7], (D_MODEL, D_FF)),
        "w_down_0": w(ks[8], (D_FF, D_MODEL)),
        "ln_mix_pre_1": gain(ks[9]),
        "ln_mix_post_1": gain(ks[10]),
        "w_in_1": w(ks[11], (D_MODEL, MOBA_IN)),
        "w_out_1": w(ks[12], (MOBA_HEADS * MOBA_DH, D_MODEL)),
        "ln_mlp_pre_1": gain(ks[13]),
        "ln_mlp_post_1": gain(ks[14]),
        "w_up_1": w(ks[15], (D_MODEL, D_FF)),
        "w_down_1": w(ks[16], (D_FF, D_MODEL)),
    }


def reference(x, ln_mix_pre_0, ln_mix_post_0, w_in_0, w_out_0, ln_mlp_pre_0, ln_mlp_post_0, w_up_0, w_down_0,
              ln_mix_pre_1, ln_mix_post_1, w_in_1, w_out_1, ln_mlp_pre_1, ln_mlp_post_1, w_up_1, w_down_1):
    mixers = (retention, moba_attention)
    layers = (
        (ln_mix_pre_0, ln_mix_post_0, w_in_0, w_out_0, ln_mlp_pre_0, ln_mlp_post_0, w_up_0, w_down_0),
        (ln_mix_pre_1, ln_mix_post_1, w_in_1, w_out_1, ln_mlp_pre_1, ln_mlp_post_1, w_up_1, w_down_1),
    )
    h = x
    for i in range(DEPTH):
        pre, post, w_in, w_out, mpre, mpost, w_up, w_down = layers[i]
        h = h + rms_norm(mixers[i % N_MIXERS](rms_norm(h, pre), w_in, w_out), post)
        h = h + rms_norm(sq_relu_mlp(rms_norm(h, mpre), w_up, w_down), mpost)
    return h
```

```python
import functools

import jax
import jax.numpy as jnp
import numpy as np
from jax import lax
from jax.experimental import pallas as pl
from jax.experimental.pallas import tpu as pltpu

D_MODEL = 1024
D_FF = 4 * D_MODEL
NORM_EPS = 1e-6
NEG = -1e30

RET_HEADS = 4
RET_DK = 256
RET_DV = 512
RET_THETA = 10000.0
RET_CHUNK = 256

MOBA_HEADS = 8
MOBA_DH = 128
MOBA_BLOCK = 256
MOBA_TOPK = 3
ROPE_THETA = 500000.0
ROPE_DIM = 32

LANES = 128
VMEM_LIMIT = 56 * 1024 * 1024

F32 = jnp.float32
BF16 = jnp.bfloat16
NT_DIMS = (((1,), (1,)), ((), ()))


def _rms_norm(x, g):
    return x * lax.rsqrt(jnp.mean(x * x, axis=-1, keepdims=True) + NORM_EPS) * g


def _inproj_kernel(x_ref, g_ref, w_ref, *rest, mode):
    tabs, (o_ref, xs_ref) = rest[:-2], rest[-2:]
    n = pl.program_id(1)

    @pl.when(n == 0)
    def _():
        xs_ref[...] = _rms_norm(x_ref[...], g_ref[...]).astype(BF16)

    acc = jnp.dot(xs_ref[...], w_ref[...], preferred_element_type=F32)
    tn = acc.shape[1]

    @pl.when(n >= 2)
    def _():
        o_ref[...] = acc.astype(o_ref.dtype)

    @pl.when(n < 2)
    def _():
        if mode == "ret":
            sc = jnp.where(n == 1, RET_DK ** -0.5, 1.0).astype(F32)
            c = tabs[0][...] * sc
            s = tabs[1][...] * sc
            for hh in range(tn // RET_DK):
                lo = hh * RET_DK
                x1 = acc[:, lo:lo + LANES]
                x2 = acc[:, lo + LANES:lo + 2 * LANES]
                o_ref[:, lo:lo + LANES] = (x1 * c - x2 * s).astype(o_ref.dtype)
                o_ref[:, lo + LANES:lo + 2 * LANES] = (x1 * s + x2 * c).astype(o_ref.dtype)
        else:
            sc = jnp.where(n == 0, MOBA_DH ** -0.5, 1.0).astype(F32)
            c = tabs[0][...] * sc
            s_next = tabs[1][...] * sc
            s_prev = tabs[2][...] * sc
            half = ROPE_DIM // 2
            for hh in range(tn // MOBA_DH):
                lo = hh * MOBA_DH
                xh = acc[:, lo:lo + LANES]
                nxt = pltpu.roll(xh, LANES - half, 1)
                prv = pltpu.roll(xh, half, 1)
                o_ref[:, lo:lo + LANES] = (xh * c + nxt * s_next + prv * s_prev).astype(o_ref.dtype)


def _inproj(x2d, gain, w, tabs, *, mode, seq, tm=1024, tn=1024):
    m_total, d = x2d.shape
    n_total = w.shape[1]
    tiles_per_seq = seq // tm
    tab_specs = [pl.BlockSpec((tm, LANES), lambda m, n: (m % tiles_per_seq, 0)) for _ in tabs]
    return pl.pallas_call(
        functools.partial(_inproj_kernel, mode=mode),
        grid=(m_total // tm, n_total // tn),
        in_specs=[
            pl.BlockSpec((tm, d), lambda m, n: (m, 0)),
            pl.BlockSpec((1, d), lambda m, n: (0, 0)),
            pl.BlockSpec((d, tn), lambda m, n: (0, n)),
            *tab_specs,
        ],
        out_specs=pl.BlockSpec((tm, tn), lambda m, n: (m, n)),
        out_shape=jax.ShapeDtypeStruct((m_total, n_total), BF16),
        scratch_shapes=[pltpu.VMEM((tm, d), BF16)],
        compiler_params=pltpu.CompilerParams(
            dimension_semantics=("arbitrary", "arbitrary"), vmem_limit_bytes=VMEM_LIMIT),
        name=f"inproj_{mode}",
    )(x2d, gain, w, *tabs)


def _retention_kernel(lg_ref, q_ref, k_ref, v_ref, g_ref, o_ref, state_ref):
    L = RET_CHUNK
    seq = q_ref.shape[1]
    lg = lg_ref[pl.program_id(1)]

    ii = lax.broadcasted_iota(jnp.int32, (L, L), 0)
    jj = lax.broadcasted_iota(jnp.int32, (L, L), 1)
    diff = (ii - jj).astype(F32)
    d_intra = jnp.where(diff >= 0, jnp.exp(lg * jnp.maximum(diff, 0.0)), 0.0)
    col = lax.broadcasted_iota(jnp.int32, (L, 1), 0).astype(F32)
    q_decay = jnp.exp(lg * (col + 1.0))
    row = lax.broadcasted_iota(jnp.int32, (1, L), 1).astype(F32)
    k_decay = jnp.exp(lg * (L - 1.0 - row))
    chunk_decay = jnp.exp(jnp.full((1, 1), lg * L, F32))

    state_ref[...] = jnp.zeros_like(state_ref)
    for c in range(seq // L):
        rows = pl.ds(c * L, L)
        qc = q_ref[0, rows, :]
        kc = k_ref[0, rows, :]
        vc = v_ref[0, rows, :]
        scores = lax.dot_general(qc, kc, NT_DIMS, preferred_element_type=F32) * d_intra
        o = jnp.dot(scores.astype(BF16), vc, preferred_element_type=F32)
        state = state_ref[...]
        o = o + jnp.dot(qc, state.astype(BF16), preferred_element_type=F32) * q_decay
        kd_t = (kc.astype(F32).T * k_decay).astype(BF16)
        state_ref[...] = state * chunk_decay + jnp.dot(kd_t, vc, preferred_element_type=F32)
        o = o * lax.rsqrt(jnp.mean(o * o, axis=-1, keepdims=True) + NORM_EPS)
        g = g_ref[0, rows, :].astype(F32)
        o_ref[0, rows, :] = (o * (g * jax.nn.sigmoid(g))).astype(o_ref.dtype)


def _retention(qkvg, log_gamma, batch, seq):
    H, DK, DV = RET_HEADS, RET_DK, RET_DV
    v_blk0 = 2 * H * DK // DV
    g_blk0 = v_blk0 + H
    grid_spec = pltpu.PrefetchScalarGridSpec(
        num_scalar_prefetch=1,
        grid=(batch, H),
        in_specs=[
            pl.BlockSpec((1, seq, DK), lambda b, h, lg: (b, 0, h)),
            pl.BlockSpec((1, seq, DK), lambda b, h, lg: (b, 0, H + h)),
            pl.BlockSpec((1, seq, DV), lambda b, h, lg: (b, 0, v_blk0 + h)),
            pl.BlockSpec((1, seq, DV), lambda b, h, lg: (b, 0, g_blk0 + h)),
        ],
        out_specs=pl.BlockSpec((1, seq, DV), lambda b, h, lg: (b, 0, h)),
        scratch_shapes=[pltpu.VMEM((DK, DV), F32)],
    )
    return pl.pallas_call(
        _retention_kernel,
        grid_spec=grid_spec,
        out_shape=jax.ShapeDtypeStruct((batch, seq, H * DV), BF16),
        compiler_params=pltpu.CompilerParams(
            dimension_semantics=("arbitrary", "arbitrary"), vmem_limit_bytes=VMEM_LIMIT),
        name="retention",
    )(log_gamma, qkvg, qkvg, qkvg, qkvg)


def _moba_kernel(q_ref, k_ref, v_ref, o_ref, vt_ref):
    BS = MOBA_BLOCK
    seq = q_ref.shape[1]
    nb = seq // BS

    for j in range(nb):
        cols = pl.ds(j * BS, BS)
        vt_ref[:, cols] = v_ref[0, cols, :].astype(F32).T.astype(BF16)

    blk_of_col = lax.broadcasted_iota(jnp.int32, (2 * nb, seq), 1) // BS
    row2 = lax.broadcasted_iota(jnp.int32, (2 * nb, seq), 0)
    avg = jnp.where(blk_of_col == row2 % nb, 1.0 / BS, 0.0).astype(BF16)
    kmean2 = jnp.dot(avg, k_ref[0], preferred_element_type=F32)
    hi = kmean2.astype(BF16).astype(F32)
    row_small = lax.broadcasted_iota(jnp.int32, kmean2.shape, 0)
    hilo = jnp.where(row_small < nb, hi, kmean2 - hi).astype(BF16)
    gate2 = lax.dot_general(hilo, q_ref[0], NT_DIMS, preferred_element_type=F32)
    gate_t = gate2[:nb] + gate2[nb:]

    blk_row = lax.broadcasted_iota(jnp.int32, (nb, BS), 0)
    key_idx = lax.broadcasted_iota(jnp.int32, (BS, BS), 0)
    qry_idx = lax.broadcasted_iota(jnp.int32, (BS, BS), 1)
    causal = key_idx <= qry_idx

    for n in range(nb):
        qrows = pl.ds(n * BS, BS)
        qn = q_ref[0, qrows, :]
        s_t = lax.dot_general(k_ref[0, qrows, :], qn, NT_DIMS, preferred_element_type=F32)
        s_t = jnp.where(causal, s_t, NEG)
        m = jnp.max(s_t, axis=0, keepdims=True)
        p = jnp.exp(s_t - m)
        l = jnp.sum(p, axis=0, keepdims=True)
        acc = jnp.dot(vt_ref[:, qrows], p.astype(BF16), preferred_element_type=F32)

        gm = jnp.where(blk_row < n, gate_t[:, n * BS:(n + 1) * BS], NEG)
        for j in range(n):
            gj = gm[j:j + 1, :]
            ahead = (gm > gj) | ((gm == gj) & (blk_row < j))
            rank = jnp.sum(ahead.astype(F32), axis=0, keepdims=True)
            sel = rank < MOBA_TOPK

            krows = pl.ds(j * BS, BS)
            s_t = lax.dot_general(k_ref[0, krows, :], qn, NT_DIMS, preferred_element_type=F32)
            m_blk = jnp.where(sel, jnp.max(s_t, axis=0, keepdims=True), NEG)
            m_new = jnp.maximum(m, m_blk)
            alpha = jnp.exp(m - m_new)
            p = jnp.exp(s_t - jnp.where(sel, m_new, -NEG))
            l = alpha * l + jnp.sum(p, axis=0, keepdims=True)
            acc = acc * alpha + jnp.dot(vt_ref[:, krows], p.astype(BF16), preferred_element_type=F32)
            m = m_new

        o_ref[0, qrows, :] = (acc * (1.0 / l)).T.astype(o_ref.dtype)


def _moba(qkv, batch, seq):
    H, DH = MOBA_HEADS, MOBA_DH
    return pl.pallas_call(
        _moba_kernel,
        grid=(batch, H),
        in_specs=[
            pl.BlockSpec((1, seq, DH), lambda b, h: (b, 0, h)),
            pl.BlockSpec((1, seq, DH), lambda b, h: (b, 0, H + h)),
            pl.BlockSpec((1, seq, DH), lambda b, h: (b, 0, 2 * H + h)),
        ],
        out_specs=pl.BlockSpec((1, seq, DH), lambda b, h: (b, 0, h)),
        out_shape=jax.ShapeDtypeStruct((batch, seq, H * DH), BF16),
        scratch_shapes=[pltpu.VMEM((DH, seq), BF16)],
        compiler_params=pltpu.CompilerParams(
            dimension_semantics=("arbitrary", "arbitrary"), vmem_limit_bytes=VMEM_LIMIT),
        name="moba",
    )(qkv, qkv, qkv)


def _outproj_mlp_kernel(a_ref, h_ref, wo_ref, gpost_ref, gpre_ref, gmpost_ref, wup_ref, wdn_ref,
                        o_ref, hid_ref, *, ff_chunk):
    mix = jnp.dot(a_ref[...], wo_ref[...], preferred_element_type=F32)
    h1 = h_ref[...] + _rms_norm(mix, gpost_ref[...])
    xn = _rms_norm(h1, gpre_ref[...]).astype(BF16)
    for j in range(wup_ref.shape[1] // ff_chunk):
        cols = pl.ds(j * ff_chunk, ff_chunk)
        u = jnp.maximum(jnp.dot(xn, wup_ref[:, cols], preferred_element_type=F32), 0.0)
        hid_ref[:, cols] = (u * u).astype(BF16)
    dn = jnp.dot(hid_ref[...], wdn_ref[...], preferred_element_type=F32)
    o_ref[...] = h1 + _rms_norm(dn, gmpost_ref[...])


def _outproj_mlp(a2d, h2d, w_out, g_post, g_mpre, g_mpost, w_up, w_down, *, tm=512, ff_chunk=1024):
    m_total, d = h2d.shape
    ka = a2d.shape[1]
    dff = w_up.shape[1]

    def resident(shape):
        return pl.BlockSpec(shape, lambda m: (0, 0), pipeline_mode=pl.Buffered(1))

    return pl.pallas_call(
        functools.partial(_outproj_mlp_kernel, ff_chunk=ff_chunk),
        grid=(m_total // tm,),
        in_specs=[
            pl.BlockSpec((tm, ka), lambda m: (m, 0)),
            pl.BlockSpec((tm, d), lambda m: (m, 0)),
            resident((ka, d)),
            resident((1, d)),
            resident((1, d)),
            resident((1, d)),
            resident((d, dff)),
            resident((dff, d)),
        ],
        out_specs=pl.BlockSpec((tm, d), lambda m: (m, 0)),
        out_shape=jax.ShapeDtypeStruct((m_total, d), F32),
        scratch_shapes=[pltpu.VMEM((tm, dff), BF16)],
        compiler_params=pltpu.CompilerParams(
            dimension_semantics=("arbitrary",), vmem_limit_bytes=VMEM_LIMIT),
        name=f"outproj_mlp_k{ka}",
    )(a2d, h2d, w_out, g_post, g_mpre, g_mpost, w_up, w_down)


def _ret_tables(seq):
    freq = 1.0 / (RET_THETA ** jnp.linspace(0.0, 1.0, RET_DK // 2, dtype=F32))
    ang = jnp.arange(seq)[:, None].astype(F32) * freq[None, :]
    return jnp.cos(ang), jnp.sin(ang)


def _moba_tables(seq):
    half = ROPE_DIM // 2
    inv = ROPE_THETA ** (-jnp.arange(0, ROPE_DIM, 2, dtype=F32) / ROPE_DIM)
    ang = jnp.arange(seq)[:, None].astype(F32) * inv[None, :]
    cos, sin = jnp.cos(ang), jnp.sin(ang)
    pad = LANES - ROPE_DIM
    c = jnp.concatenate([cos, cos, jnp.ones((seq, pad), F32)], axis=1)
    s_next = jnp.concatenate([-sin, jnp.zeros((seq, half + pad), F32)], axis=1)
    s_prev = jnp.concatenate([jnp.zeros((seq, half), F32), sin, jnp.zeros((seq, pad), F32)], axis=1)
    return c, s_next, s_prev


def _deinterleave_heads(w, n_heads, dk):
    d = w.shape[0]
    return w.reshape(d, n_heads, dk // 2, 2).transpose(0, 1, 3, 2).reshape(d, n_heads * dk)


def kernel(x, ln_mix_pre_0, ln_mix_post_0, w_in_0, w_out_0, ln_mlp_pre_0, ln_mlp_post_0, w_up_0, w_down_0,
           ln_mix_pre_1, ln_mix_post_1, w_in_1, w_out_1, ln_mlp_pre_1, ln_mlp_post_1, w_up_1, w_down_1):
    batch, seq, d = x.shape
    row = lambda g: g.reshape(1, d).astype(F32)

    n_qk = 2 * RET_HEADS * RET_DK
    w_in_0p = jnp.concatenate(
        [_deinterleave_heads(w_in_0[:, :n_qk], 2 * RET_HEADS, RET_DK), w_in_0[:, n_qk:]], axis=1).astype(BF16)
    log_gamma = jnp.log(1.0 - 2.0 ** (-5.0 - jnp.arange(RET_HEADS, dtype=F32)))

    h = x.reshape(batch * seq, d)

    qkvg = _inproj(h, row(ln_mix_pre_0), w_in_0p, _ret_tables(seq), mode="ret", seq=seq)
    mix = _retention(qkvg.reshape(batch, seq, -1), log_gamma, batch, seq)
    h = _outproj_mlp(mix.reshape(batch * seq, -1), h, w_out_0.astype(BF16), row(ln_mix_post_0),
                     row(ln_mlp_pre_0), row(ln_mlp_post_0), w_up_0.astype(BF16), w_down_0.astype(BF16))

    qkv = _inproj(h, row(ln_mix_pre_1), w_in_1.astype(BF16), _moba_tables(seq), mode="moba", seq=seq)
    mix = _moba(qkv.reshape(batch, seq, -1), batch, seq)
    h = _outproj_mlp(mix.reshape(batch * seq, -1), h, w_out_1.astype(BF16), row(ln_mix_post_1),
                     row(ln_mlp_pre_1), row(ln_mlp_post_1), w_up_1.astype(BF16), w_down_1.astype(BF16))
    return h.reshape(batch, seq, d)
```

```python
import functools
import math

import jax
import jax.numpy as jnp
from jax import lax
from jax.experimental import pallas as pl
from jax.experimental.pallas import tpu as pltpu

D_MODEL = 1024
D_FF = 4 * D_MODEL
NORM_EPS = 1e-6
NEG = -1e30
LOG2E = math.log2(math.e)

RET_HEADS = 4
RET_DK = 256
RET_DV = 512
RET_THETA = 10000.0
RET_CHUNK = 256

MOBA_HEADS = 8
MOBA_DH = 128
MOBA_BLOCK = 256
MOBA_TOPK = 3
ROPE_THETA = 500000.0
ROPE_DIM = 32

LANES = 128
VMEM_LIMIT = 56 * 1024 * 1024

F32 = jnp.float32
BF16 = jnp.bfloat16
NT_DIMS = (((1,), (1,)), ((), ()))


def _rms_norm(x, g):
    return x * lax.rsqrt(jnp.mean(x * x, axis=-1, keepdims=True) + NORM_EPS) * g


def _resident(shape):
    return pl.BlockSpec(shape, lambda *_: (0,) * len(shape), pipeline_mode=pl.Buffered(1))


def _inproj_plan(mode):
    if mode == "ret":
        hq = RET_HEADS * RET_DK
        plan = [(h * RET_DK, RET_DK, 1.0) for h in range(RET_HEADS)]
        plan += [(hq + h * RET_DK, RET_DK, RET_DK ** -0.5) for h in range(RET_HEADS)]
        plan += [(2 * hq + c * RET_DV, RET_DV, None) for c in range(2 * RET_HEADS)]
    else:
        hq = MOBA_HEADS * MOBA_DH
        cw = 4 * MOBA_DH
        plan = [(c * cw, cw, MOBA_DH ** -0.5 * LOG2E) for c in range(hq // cw)]
        plan += [(hq + c * cw, cw, 1.0) for c in range(hq // cw)]
        plan += [(2 * hq + c * cw, cw, None) for c in range(hq // cw)]
    return plan


def _inproj_kernel(x_ref, g_ref, w_ref, cos_ref, sin_ref, o_ref, xs_ref, *, mode):
    xs_ref[...] = _rms_norm(x_ref[...], g_ref[...]).astype(BF16)
    cos, sin = cos_ref[...], sin_ref[...]
    scaled = {}
    for start, width, scale in _inproj_plan(mode):
        acc = jnp.dot(xs_ref[...], w_ref[:, start:start + width], preferred_element_type=F32)
        if scale is None:
            o_ref[:, start:start + width] = acc.astype(o_ref.dtype)
            continue
        if scale not in scaled:
            scaled[scale] = (cos * scale, sin * scale)
        c, s = scaled[scale]
        if mode == "ret":
            x1, x2 = acc[:, :LANES], acc[:, LANES:]
            o_ref[:, start:start + LANES] = (x1 * c - x2 * s).astype(o_ref.dtype)
            o_ref[:, start + LANES:start + width] = (x1 * s + x2 * c).astype(o_ref.dtype)
        else:
            for lo in range(0, width, LANES):
                xh = acc[:, lo:lo + LANES]
                y = xh * c + pltpu.roll(xh, LANES // 2, 1) * s
                o_ref[:, start + lo:start + lo + LANES] = y.astype(o_ref.dtype)


def _inproj(x2d, gain, w, tabs, *, mode, seq, tm=512):
    m_total, d = x2d.shape
    n_total = w.shape[1]
    tiles_per_seq = seq // tm
    tab_spec = pl.BlockSpec((tm, LANES), lambda m: (m % tiles_per_seq, 0))
    return pl.pallas_call(
        functools.partial(_inproj_kernel, mode=mode),
        grid=(m_total // tm,),
        in_specs=[pl.BlockSpec((tm, d), lambda m: (m, 0)), _resident((1, d)), _resident((d, n_total)),
                  tab_spec, tab_spec],
        out_specs=pl.BlockSpec((tm, n_total), lambda m: (m, 0)),
        out_shape=jax.ShapeDtypeStruct((m_total, n_total), BF16),
        scratch_shapes=[pltpu.VMEM((tm, d), BF16)],
        compiler_params=pltpu.CompilerParams(
            dimension_semantics=("arbitrary",), vmem_limit_bytes=VMEM_LIMIT),
        name=f"inproj_{mode}",
    )(x2d, gain, w, *tabs)


def _retention_kernel(lg_ref, q_ref, k_ref, v_ref, g_ref, o_ref, state_ref):
    L = RET_CHUNK
    seq = q_ref.shape[1]
    lg = lg_ref[pl.program_id(1)]

    ii = lax.broadcasted_iota(jnp.int32, (L, L), 0)
    jj = lax.broadcasted_iota(jnp.int32, (L, L), 1)
    diff = (ii - jj).astype(F32)
    d_intra = jnp.where(diff >= 0, jnp.exp(lg * jnp.maximum(diff, 0.0)), 0.0)
    col = lax.broadcasted_iota(jnp.int32, (L, 1), 0).astype(F32)
    q_decay = jnp.exp(lg * (col + 1.0))
    row = lax.broadcasted_iota(jnp.int32, (1, L), 1).astype(F32)
    k_decay = jnp.exp(lg * (L - 1.0 - row))
    chunk_decay = jnp.exp(jnp.full((1, 1), lg * L, F32))

    state_ref[...] = jnp.zeros_like(state_ref)
    for c in range(seq // L):
        rows = pl.ds(c * L, L)
        qc = q_ref[0, rows, :]
        kc = k_ref[0, rows, :]
        vc = v_ref[0, rows, :]
        scores = lax.dot_general(qc, kc, NT_DIMS, preferred_element_type=F32) * d_intra
        o = jnp.dot(scores.astype(BF16), vc, preferred_element_type=F32)
        state = state_ref[...]
        o = o + jnp.dot(qc, state.astype(BF16), preferred_element_type=F32) * q_decay
        kd_t = (kc.astype(F32).T * k_decay).astype(BF16)
        state_ref[...] = state * chunk_decay + jnp.dot(kd_t, vc, preferred_element_type=F32)
        o = o * lax.rsqrt(jnp.mean(o * o, axis=-1, keepdims=True) + NORM_EPS)
        g = g_ref[0, rows, :].astype(F32)
        o_ref[0, rows, :] = (o * (g * jax.nn.sigmoid(g))).astype(o_ref.dtype)


def _retention(qkvg, log_gamma, batch, seq):
    H, DK, DV = RET_HEADS, RET_DK, RET_DV
    v_blk0 = 2 * H * DK // DV
    g_blk0 = v_blk0 + H
    grid_spec = pltpu.PrefetchScalarGridSpec(
        num_scalar_prefetch=1,
        grid=(batch, H),
        in_specs=[
            pl.BlockSpec((1, seq, DK), lambda b, h, lg: (b, 0, h)),
            pl.BlockSpec((1, seq, DK), lambda b, h, lg: (b, 0, H + h)),
            pl.BlockSpec((1, seq, DV), lambda b, h, lg: (b, 0, v_blk0 + h)),
            pl.BlockSpec((1, seq, DV), lambda b, h, lg: (b, 0, g_blk0 + h)),
        ],
        out_specs=pl.BlockSpec((1, seq, DV), lambda b, h, lg: (b, 0, h)),
        scratch_shapes=[pltpu.VMEM((DK, DV), F32)],
    )
    return pl.pallas_call(
        _retention_kernel,
        grid_spec=grid_spec,
        out_shape=jax.ShapeDtypeStruct((batch, seq, H * DV), BF16),
        compiler_params=pltpu.CompilerParams(
            dimension_semantics=("arbitrary", "arbitrary"), vmem_limit_bytes=VMEM_LIMIT),
        name="retention",
    )(log_gamma, qkvg, qkvg, qkvg, qkvg)


def _moba_kernel(q_ref, k_ref, v_ref, o_ref, vt_ref, s_ref, p_ref):
    BS = MOBA_BLOCK
    seq = q_ref.shape[1]
    nb = seq // BS

    for j in range(nb):
        cols = pl.ds(j * BS, BS)
        vt_ref[:, cols] = v_ref[0, cols, :].astype(F32).T.astype(BF16)

    blk_of_col = lax.broadcasted_iota(jnp.int32, (2 * nb, seq), 1) // BS
    row2 = lax.broadcasted_iota(jnp.int32, (2 * nb, seq), 0)
    avg = jnp.where(blk_of_col == row2 % nb, 1.0 / BS, 0.0).astype(BF16)
    kmean2 = jnp.dot(avg, k_ref[0], preferred_element_type=F32)
    hi = kmean2.astype(BF16).astype(F32)
    row_small = lax.broadcasted_iota(jnp.int32, kmean2.shape, 0)
    hilo = jnp.where(row_small < nb, hi, kmean2 - hi).astype(BF16)
    gate2 = lax.dot_general(hilo, q_ref[0], NT_DIMS, preferred_element_type=F32)
    gate_t = gate2[:nb] + gate2[nb:]

    blk_row = lax.broadcasted_iota(jnp.int32, (nb, BS), 0)
    key_idx = lax.broadcasted_iota(jnp.int32, (BS, BS), 0)
    qry_idx = lax.broadcasted_iota(jnp.int32, (BS, BS), 1)
    causal = key_idx <= qry_idx

    def scores(n):
        nk = (n + 1) * BS
        s_ref[n % 2, 0:nk, :] = lax.dot_general(k_ref[0, 0:nk, :], q_ref[0, pl.ds(n * BS, BS), :], NT_DIMS,
                                                preferred_element_type=F32)

    scores(0)
    for n in range(nb):
        qrows = pl.ds(n * BS, BS)
        nk = (n + 1) * BS
        if n + 1 < nb:
            scores(n + 1)
        s_cur, p_cur = s_ref.at[n % 2], p_ref.at[n % 2]

        gm = jnp.where(blk_row < n, gate_t[:, n * BS:(n + 1) * BS], NEG)
        sels = []
        for j in range(n):
            gj = gm[j:j + 1, :]
            ahead = (gm > gj) | ((gm == gj) & (blk_row < j))
            rank = jnp.sum(ahead.astype(F32), axis=0, keepdims=True)
            sels.append(rank < MOBA_TOPK)

        s_own = jnp.where(causal, s_cur[pl.ds(n * BS, BS), :], NEG)
        m = jnp.max(s_own, axis=0, keepdims=True)
        for j in range(n):
            m_blk = jnp.max(s_cur[pl.ds(j * BS, BS), :], axis=0, keepdims=True)
            m = jnp.maximum(m, jnp.where(sels[j], m_blk, NEG))

        p = jnp.exp2(s_own - m)
        l = jnp.sum(p, axis=0, keepdims=True)
        p_cur[pl.ds(n * BS, BS), :] = p.astype(BF16)
        for j in range(n):
            p = jnp.exp2(s_cur[pl.ds(j * BS, BS), :] - jnp.where(sels[j], m, -NEG))
            l = l + jnp.sum(p, axis=0, keepdims=True)
            p_cur[pl.ds(j * BS, BS), :] = p.astype(BF16)

        acc = jnp.dot(vt_ref[:, 0:nk], p_cur[0:nk, :], preferred_element_type=F32)
        o_ref[0, qrows, :] = (acc * (1.0 / l)).T.astype(o_ref.dtype)


def _moba(qkv, batch, seq):
    H, DH = MOBA_HEADS, MOBA_DH
    return pl.pallas_call(
        _moba_kernel,
        grid=(batch, H),
        in_specs=[
            pl.BlockSpec((1, seq, DH), lambda b, h: (b, 0, h)),
            pl.BlockSpec((1, seq, DH), lambda b, h: (b, 0, H + h)),
            pl.BlockSpec((1, seq, DH), lambda b, h: (b, 0, 2 * H + h)),
        ],
        out_specs=pl.BlockSpec((1, seq, DH), lambda b, h: (b, 0, h)),
        out_shape=jax.ShapeDtypeStruct((batch, seq, H * DH), BF16),
        scratch_shapes=[pltpu.VMEM((DH, seq), BF16),
                        pltpu.VMEM((2, seq, MOBA_BLOCK), F32),
                        pltpu.VMEM((2, seq, MOBA_BLOCK), BF16)],
        compiler_params=pltpu.CompilerParams(
            dimension_semantics=("arbitrary", "arbitrary"), vmem_limit_bytes=VMEM_LIMIT),
        name="moba",
    )(qkv, qkv, qkv)


def _outproj_mlp_kernel(a_ref, h_ref, wo_ref, gpost_ref, gpre_ref, gmpost_ref, wup_ref, wdn_ref,
                        o_ref, hid_ref, *, ff_chunk):
    mix = jnp.dot(a_ref[...], wo_ref[...], preferred_element_type=F32)
    h1 = h_ref[...] + _rms_norm(mix, gpost_ref[...])
    xn = _rms_norm(h1, gpre_ref[...]).astype(BF16)
    for j in range(wup_ref.shape[1] // ff_chunk):
        cols = pl.ds(j * ff_chunk, ff_chunk)
        u = jnp.maximum(jnp.dot(xn, wup_ref[:, cols], preferred_element_type=F32), 0.0)
        hid_ref[:, cols] = (u * u).astype(BF16)
    dn = jnp.dot(hid_ref[...], wdn_ref[...], preferred_element_type=F32)
    o_ref[...] = h1 + _rms_norm(dn, gmpost_ref[...])


def _outproj_mlp(a2d, h2d, w_out, g_post, g_mpre, g_mpost, w_up, w_down, *, tm=512, ff_chunk=1024):
    m_total, d = h2d.shape
    ka = a2d.shape[1]
    dff = w_up.shape[1]
    return pl.pallas_call(
        functools.partial(_outproj_mlp_kernel, ff_chunk=ff_chunk),
        grid=(m_total // tm,),
        in_specs=[
            pl.BlockSpec((tm, ka), lambda m: (m, 0)),
            pl.BlockSpec((tm, d), lambda m: (m, 0)),
            _resident((ka, d)),
            _resident((1, d)),
            _resident((1, d)),
            _resident((1, d)),
            _resident((d, dff)),
            _resident((dff, d)),
        ],
        out_specs=pl.BlockSpec((tm, d), lambda m: (m, 0)),
        out_shape=jax.ShapeDtypeStruct((m_total, d), F32),
        scratch_shapes=[pltpu.VMEM((tm, dff), BF16)],
        compiler_params=pltpu.CompilerParams(
            dimension_semantics=("arbitrary",), vmem_limit_bytes=VMEM_LIMIT),
        name=f"outproj_mlp_k{ka}",
    )(a2d, h2d, w_out, g_post, g_mpre, g_mpost, w_up, w_down)


def _ret_tables(seq):
    freq = 1.0 / (RET_THETA ** jnp.linspace(0.0, 1.0, RET_DK // 2, dtype=F32))
    ang = jnp.arange(seq)[:, None].astype(F32) * freq[None, :]
    return jnp.cos(ang), jnp.sin(ang)


def _moba_tables(seq):
    half = ROPE_DIM // 2
    inv = ROPE_THETA ** (-jnp.arange(0, ROPE_DIM, 2, dtype=F32) / ROPE_DIM)
    ang = jnp.arange(seq)[:, None].astype(F32) * inv[None, :]
    cos, sin = jnp.cos(ang), jnp.sin(ang)
    gap = LANES // 2 - half
    c = jnp.concatenate([cos, jnp.ones((seq, gap), F32), cos, jnp.ones((seq, gap), F32)], axis=1)
    s = jnp.concatenate([-sin, jnp.zeros((seq, gap), F32), sin, jnp.zeros((seq, gap), F32)], axis=1)
    return c, s


def _deinterleave_heads(w, n_heads, dk):
    d = w.shape[0]
    return w.reshape(d, n_heads, dk // 2, 2).transpose(0, 1, 3, 2).reshape(d, n_heads * dk)


def _spread_rope_halves(w, n_heads, dh):
    d = w.shape[0]
    half = ROPE_DIM // 2
    gap = dh // 2 - half
    w = w.reshape(d, n_heads, dh)
    x1, x2, rest = w[..., :half], w[..., half:ROPE_DIM], w[..., ROPE_DIM:]
    return jnp.concatenate([x1, rest[..., :gap], x2, rest[..., gap:]], axis=-1).reshape(d, n_heads * dh)


def kernel(x, ln_mix_pre_0, ln_mix_post_0, w_in_0, w_out_0, ln_mlp_pre_0, ln_mlp_post_0, w_up_0, w_down_0,
           ln_mix_pre_1, ln_mix_post_1, w_in_1, w_out_1, ln_mlp_pre_1, ln_mlp_post_1, w_up_1, w_down_1):
    batch, seq, d = x.shape
    row = lambda g: g.reshape(1, d).astype(F32)

    n_qk0 = 2 * RET_HEADS * RET_DK
    w_in_0p = jnp.concatenate(
        [_deinterleave_heads(w_in_0[:, :n_qk0], 2 * RET_HEADS, RET_DK), w_in_0[:, n_qk0:]], axis=1).astype(BF16)
    n_qk1 = 2 * MOBA_HEADS * MOBA_DH
    w_in_1p = jnp.concatenate(
        [_spread_rope_halves(w_in_1[:, :n_qk1], 2 * MOBA_HEADS, MOBA_DH), w_in_1[:, n_qk1:]], axis=1).astype(BF16)
    log_gamma = jnp.log(1.0 - 2.0 ** (-5.0 - jnp.arange(RET_HEADS, dtype=F32)))

    h = x.reshape(batch * seq, d)

    qkvg = _inproj(h, row(ln_mix_pre_0), w_in_0p, _ret_tables(seq), mode="ret", seq=seq)
    mix = _retention(qkvg.reshape(batch, seq, -1), log_gamma, batch, seq)
    h = _outproj_mlp(mix.reshape(batch * seq, -1), h, w_out_0.astype(BF16), row(ln_mix_post_0),
                     row(ln_mlp_pre_0), row(ln_mlp_post_0), w_up_0.astype(BF16), w_down_0.astype(BF16))

    qkv = _inproj(h, row(ln_mix_pre_1), w_in_1p, _moba_tables(seq), mode="moba", seq=seq)
    mix = _moba(qkv.reshape(batch, seq, -1), batch, seq)
    h = _outproj_mlp(mix.reshape(batch * seq, -1), h, w_out_1.astype(BF16), row(ln_mix_post_1),
                     row(ln_mlp_pre_1), row(ln_mlp_post_1), w_up_1.astype(BF16), w_down_1.astype(BF16))
    return h.reshape(batch, seq, d)
```

```python
import functools
import math

import jax
import jax.numpy as jnp
from jax import lax
from jax.experimental import pallas as pl
from jax.experimental.pallas import tpu as pltpu

D_MODEL = 1024
D_FF = 4 * D_MODEL
NORM_EPS = 1e-6
NEG = -1e30
LOG2E = math.log2(math.e)

RET_HEADS = 4
RET_DK = 256
RET_DV = 512
RET_THETA = 10000.0
RET_CHUNK = 256

MOBA_HEADS = 8
MOBA_DH = 128
MOBA_BLOCK = 256
MOBA_TOPK = 3
ROPE_THETA = 500000.0
ROPE_DIM = 32

LANES = 128
BF16_SUBLANES = 16
VMEM_LIMIT = 56 * 1024 * 1024

F32 = jnp.float32
BF16 = jnp.bfloat16
NT_DIMS = (((1,), (1,)), ((), ()))


def _rms_norm(x, g):
    return x * lax.rsqrt(jnp.mean(x * x, axis=-1, keepdims=True) + NORM_EPS) * g


def _resident(shape):
    return pl.BlockSpec(shape, lambda *_: (0,) * len(shape), pipeline_mode=pl.Buffered(1))


def _inproj_plan(mode):
    if mode == "ret":
        hq = RET_HEADS * RET_DK
        plan = [(h * RET_DK, RET_DK, 1.0) for h in range(RET_HEADS)]
        plan += [(hq + h * RET_DK, RET_DK, RET_DK ** -0.5) for h in range(RET_HEADS)]
        plan += [(2 * hq + c * RET_DV, RET_DV, None) for c in range(2 * RET_HEADS)]
    else:
        hq = MOBA_HEADS * MOBA_DH
        cw = 4 * MOBA_DH
        plan = [(c * cw, cw, MOBA_DH ** -0.5 * LOG2E) for c in range(hq // cw)]
        plan += [(hq + c * cw, cw, 1.0) for c in range(hq // cw)]
        plan += [(2 * hq + c * cw, cw, None) for c in range(hq // cw)]
    return plan


def _inproj_kernel(x_ref, g_ref, w_ref, cos_ref, sin_ref, o_ref, xs_ref, *, mode):
    xs_ref[...] = _rms_norm(x_ref[...], g_ref[...]).astype(BF16)
    cos, sin = cos_ref[...], sin_ref[...]
    scaled = {}
    for start, width, scale in _inproj_plan(mode):
        acc = jnp.dot(xs_ref[...], w_ref[:, start:start + width], preferred_element_type=F32)
        if scale is None:
            o_ref[:, start:start + width] = acc.astype(o_ref.dtype)
            continue
        if scale not in scaled:
            scaled[scale] = (cos * scale, sin * scale)
        c, s = scaled[scale]
        if mode == "ret":
            x1, x2 = acc[:, :LANES], acc[:, LANES:]
            o_ref[:, start:start + LANES] = (x1 * c - x2 * s).astype(o_ref.dtype)
            o_ref[:, start + LANES:start + width] = (x1 * s + x2 * c).astype(o_ref.dtype)
        else:
            for lo in range(0, width, LANES):
                xh = acc[:, lo:lo + LANES]
                y = xh * c + pltpu.roll(xh, LANES // 2, 1) * s
                o_ref[:, start + lo:start + lo + LANES] = y.astype(o_ref.dtype)


def _inproj(x2d, gain, w, tabs, *, mode, seq, tm=512):
    m_total, d = x2d.shape
    n_total = w.shape[1]
    tiles_per_seq = seq // tm
    tab_spec = pl.BlockSpec((tm, LANES), lambda m: (m % tiles_per_seq, 0))
    return pl.pallas_call(
        functools.partial(_inproj_kernel, mode=mode),
        grid=(m_total // tm,),
        in_specs=[pl.BlockSpec((tm, d), lambda m: (m, 0)), _resident((1, d)), _resident((d, n_total)),
                  tab_spec, tab_spec],
        out_specs=pl.BlockSpec((tm, n_total), lambda m: (m, 0)),
        out_shape=jax.ShapeDtypeStruct((m_total, n_total), BF16),
        scratch_shapes=[pltpu.VMEM((tm, d), BF16)],
        compiler_params=pltpu.CompilerParams(
            dimension_semantics=("arbitrary",), vmem_limit_bytes=VMEM_LIMIT),
        name=f"inproj_{mode}",
    )(x2d, gain, w, *tabs)


def _retention_kernel(lg_ref, q_ref, k_ref, v_ref, g_ref, o_ref, state_ref):
    L = RET_CHUNK
    seq = q_ref.shape[1]
    lg = lg_ref[pl.program_id(1)]

    ii = lax.broadcasted_iota(jnp.int32, (L, L), 0)
    jj = lax.broadcasted_iota(jnp.int32, (L, L), 1)
    diff = (ii - jj).astype(F32)
    d_intra = jnp.where(diff >= 0, jnp.exp(lg * jnp.maximum(diff, 0.0)), 0.0)
    col = lax.broadcasted_iota(jnp.int32, (L, 1), 0).astype(F32)
    q_decay = jnp.exp(lg * (col + 1.0))
    row = lax.broadcasted_iota(jnp.int32, (1, L), 1).astype(F32)
    k_decay = jnp.exp(lg * (L - 1.0 - row))
    chunk_decay = jnp.exp(jnp.full((1, 1), lg * L, F32))

    state_ref[...] = jnp.zeros_like(state_ref)
    for c in range(seq // L):
        rows = pl.ds(c * L, L)
        qc = q_ref[0, rows, :]
        kc = k_ref[0, rows, :]
        vc = v_ref[0, rows, :]
        scores = lax.dot_general(qc, kc, NT_DIMS, preferred_element_type=F32) * d_intra
        o = jnp.dot(scores.astype(BF16), vc, preferred_element_type=F32)
        state = state_ref[...]
        o = o + jnp.dot(qc, state.astype(BF16), preferred_element_type=F32) * q_decay
        kd_t = (kc.astype(F32).T * k_decay).astype(BF16)
        state_ref[...] = state * chunk_decay + jnp.dot(kd_t, vc, preferred_element_type=F32)
        o = o * lax.rsqrt(jnp.mean(o * o, axis=-1, keepdims=True) + NORM_EPS)
        g = g_ref[0, rows, :].astype(F32)
        o_ref[0, rows, :] = (o * (g * jax.nn.sigmoid(g))).astype(o_ref.dtype)


def _retention(qkvg, log_gamma, batch, seq):
    H, DK, DV = RET_HEADS, RET_DK, RET_DV
    v_blk0 = 2 * H * DK // DV
    g_blk0 = v_blk0 + H
    grid_spec = pltpu.PrefetchScalarGridSpec(
        num_scalar_prefetch=1,
        grid=(batch, H),
        in_specs=[
            pl.BlockSpec((1, seq, DK), lambda b, h, lg: (b, 0, h)),
            pl.BlockSpec((1, seq, DK), lambda b, h, lg: (b, 0, H + h)),
            pl.BlockSpec((1, seq, DV), lambda b, h, lg: (b, 0, v_blk0 + h)),
            pl.BlockSpec((1, seq, DV), lambda b, h, lg: (b, 0, g_blk0 + h)),
        ],
        out_specs=pl.BlockSpec((1, seq, DV), lambda b, h, lg: (b, 0, h)),
        scratch_shapes=[pltpu.VMEM((DK, DV), F32)],
    )
    return pl.pallas_call(
        _retention_kernel,
        grid_spec=grid_spec,
        out_shape=jax.ShapeDtypeStruct((batch, seq, H * DV), BF16),
        compiler_params=pltpu.CompilerParams(
            dimension_semantics=("arbitrary", "arbitrary"), vmem_limit_bytes=VMEM_LIMIT),
        name="retention",
    )(log_gamma, qkvg, qkvg, qkvg, qkvg)


def _moba_kernel(q_ref, k_ref, v_ref, o_ref, vt_ref, s0_ref, s1_ref, p0_ref, p1_ref):
    s_bufs, p_bufs = (s0_ref, s1_ref), (p0_ref, p1_ref)
    BS = MOBA_BLOCK
    seq = q_ref.shape[1]
    nb = seq // BS

    QT = 2
    order = list(range(nb // QT - 1, -1, -1))

    def scores(i):
        t = order[i]
        nk = (t + 1) * QT * BS
        s_bufs[i % 2][0:nk, :] = lax.dot_general(k_ref[0, 0:nk, :], q_ref[0, pl.ds(t * QT * BS, QT * BS), :],
                                                 NT_DIMS, preferred_element_type=F32)

    scores(0)

    dh = v_ref.shape[2]
    for j in range(nb):
        cols = pl.ds(j * BS, BS)
        vt_ref[0:dh, cols] = v_ref[0, cols, :].astype(F32).T.astype(BF16)
    vt_ref[dh:, :] = jnp.ones((vt_ref.shape[0] - dh, seq), BF16)

    gate_floor = NEG * BS * LOG2E
    key_idx = lax.broadcasted_iota(jnp.int32, (BS, BS), 0)
    qry_idx = lax.broadcasted_iota(jnp.int32, (BS, BS), 1)
    causal = key_idx <= qry_idx

    for i, t in enumerate(order):
        nk = (t + 1) * QT * BS
        if i + 1 < len(order):
            scores(i + 1)
        s_cur, p_cur = s_bufs[i % 2], p_bufs[i % 2]

        for hf in range(QT):
            n = t * QT + hf
            cs = pl.ds(hf * BS, BS)

            cmax, gate = [], []
            for j in range(n):
                blk = s_cur[pl.ds(j * BS, BS), cs]
                cmax.append(jnp.max(blk, axis=0, keepdims=True))
                gate.append(jnp.sum(blk, axis=0, keepdims=True))

            sels = []
            for j in range(n):
                rank = jnp.where(gate[j] < gate_floor, float(nb - n), 0.0)
                for i in range(n):
                    if i != j:
                        ahead = (gate[i] >= gate[j]) if i < j else (gate[i] > gate[j])
                        rank = rank + ahead.astype(F32)
                sels.append(rank < MOBA_TOPK)

            s_own = jnp.where(causal, s_cur[pl.ds(n * BS, BS), cs], NEG)
            m = jnp.max(s_own, axis=0, keepdims=True)
            for j in range(n):
                m = jnp.maximum(m, jnp.where(sels[j], cmax[j], NEG))

            p_cur[pl.ds(n * BS, BS), cs] = jnp.exp2(s_own - m).astype(BF16)
            for j in range(n):
                p = jnp.exp2(s_cur[pl.ds(j * BS, BS), cs] - jnp.where(sels[j], m, -NEG))
                p_cur[pl.ds(j * BS, BS), cs] = p.astype(BF16)
            for j in range(n + 1, (t + 1) * QT):
                p_cur[pl.ds(j * BS, BS), cs] = jnp.zeros((BS, BS), BF16)

        acc = jnp.dot(vt_ref[:, 0:nk], p_cur[0:nk, :], preferred_element_type=F32)
        for hf in range(QT):
            cs = slice(hf * BS, (hf + 1) * BS)
            out = acc[0:dh, cs] * (1.0 / acc[dh:dh + 1, cs])
            o_ref[0, pl.ds((t * QT + hf) * BS, BS), :] = out.T.astype(o_ref.dtype)


def _moba(qkv, batch, seq):
    H, DH = MOBA_HEADS, MOBA_DH
    return pl.pallas_call(
        _moba_kernel,
        grid=(batch, H),
        in_specs=[
            pl.BlockSpec((1, seq, DH), lambda b, h: (b, 0, h)),
            pl.BlockSpec((1, seq, DH), lambda b, h: (b, 0, H + h)),
            pl.BlockSpec((1, seq, DH), lambda b, h: (b, 0, 2 * H + h)),
        ],
        out_specs=pl.BlockSpec((1, seq, DH), lambda b, h: (b, 0, h)),
        out_shape=jax.ShapeDtypeStruct((batch, seq, H * DH), BF16),
        scratch_shapes=[pltpu.VMEM((DH + BF16_SUBLANES, seq), BF16),
                        pltpu.VMEM((seq, 2 * MOBA_BLOCK), F32), pltpu.VMEM((seq, 2 * MOBA_BLOCK), F32),
                        pltpu.VMEM((seq, 2 * MOBA_BLOCK), BF16), pltpu.VMEM((seq, 2 * MOBA_BLOCK), BF16)],
        compiler_params=pltpu.CompilerParams(
            dimension_semantics=("arbitrary", "arbitrary"), vmem_limit_bytes=VMEM_LIMIT),
        name="moba",
    )(qkv, qkv, qkv)


def _outproj_mlp_kernel(a_ref, h_ref, wo_ref, gpost_ref, gpre_ref, gmpost_ref, wup_ref, wdn_ref,
                        o_ref, xn_ref, hid_ref, *, sub, ff_chunk):
    subs = [pl.ds(r, sub) for r in range(0, a_ref.shape[0], sub)]
    for rows in subs:
        mix = jnp.dot(a_ref[rows, :], wo_ref[...], preferred_element_type=F32)
        h1 = h_ref[rows, :] + _rms_norm(mix, gpost_ref[...])
        o_ref[rows, :] = h1
        xn_ref[rows, :] = _rms_norm(h1, gpre_ref[...]).astype(BF16)
    for rows in subs:
        for j in range(wup_ref.shape[1] // ff_chunk):
            cols = pl.ds(j * ff_chunk, ff_chunk)
            u = jnp.maximum(jnp.dot(xn_ref[rows, :], wup_ref[:, cols], preferred_element_type=F32), 0.0)
            hid_ref[rows, cols] = (u * u).astype(BF16)
    for rows in subs:
        dn = jnp.dot(hid_ref[rows, :], wdn_ref[...], preferred_element_type=F32)
        o_ref[rows, :] = o_ref[rows, :] + _rms_norm(dn, gmpost_ref[...])


def _outproj_mlp(a2d, h2d, w_out, g_post, g_mpre, g_mpost, w_up, w_down, *, tm=512, sub=256, ff_chunk=1024):
    m_total, d = h2d.shape
    ka = a2d.shape[1]
    dff = w_up.shape[1]
    return pl.pallas_call(
        functools.partial(_outproj_mlp_kernel, sub=sub, ff_chunk=ff_chunk),
        grid=(m_total // tm,),
        in_specs=[
            pl.BlockSpec((tm, ka), lambda m: (m, 0)),
            pl.BlockSpec((tm, d), lambda m: (m, 0)),
            _resident((ka, d)),
            _resident((1, d)),
            _resident((1, d)),
            _resident((1, d)),
            _resident((d, dff)),
            _resident((dff, d)),
        ],
        out_specs=pl.BlockSpec((tm, d), lambda m: (m, 0)),
        out_shape=jax.ShapeDtypeStruct((m_total, d), F32),
        scratch_shapes=[pltpu.VMEM((tm, d), BF16), pltpu.VMEM((tm, dff), BF16)],
        compiler_params=pltpu.CompilerParams(
            dimension_semantics=("arbitrary",), vmem_limit_bytes=VMEM_LIMIT),
        name=f"outproj_mlp_k{ka}",
    )(a2d, h2d, w_out, g_post, g_mpre, g_mpost, w_up, w_down)


def _ret_tables(seq):
    freq = 1.0 / (RET_THETA ** jnp.linspace(0.0, 1.0, RET_DK // 2, dtype=F32))
    ang = jnp.arange(seq)[:, None].astype(F32) * freq[None, :]
    return jnp.cos(ang), jnp.sin(ang)


def _moba_tables(seq):
    half = ROPE_DIM // 2
    inv = ROPE_THETA ** (-jnp.arange(0, ROPE_DIM, 2, dtype=F32) / ROPE_DIM)
    ang = jnp.arange(seq)[:, None].astype(F32) * inv[None, :]
    cos, sin = jnp.cos(ang), jnp.sin(ang)
    gap = LANES // 2 - half
    c = jnp.concatenate([cos, jnp.ones((seq, gap), F32), cos, jnp.ones((seq, gap), F32)], axis=1)
    s = jnp.concatenate([-sin, jnp.zeros((seq, gap), F32), sin, jnp.zeros((seq, gap), F32)], axis=1)
    return c, s


def _deinterleave_heads(w, n_heads, dk):
    d = w.shape[0]
    return w.reshape(d, n_heads, dk // 2, 2).transpose(0, 1, 3, 2).reshape(d, n_heads * dk)


def _spread_rope_halves(w, n_heads, dh):
    d = w.shape[0]
    half = ROPE_DIM // 2
    gap = dh // 2 - half
    w = w.reshape(d, n_heads, dh)
    x1, x2, rest = w[..., :half], w[..., half:ROPE_DIM], w[..., ROPE_DIM:]
    return jnp.concatenate([x1, rest[..., :gap], x2, rest[..., gap:]], axis=-1).reshape(d, n_heads * dh)


def kernel(x, ln_mix_pre_0, ln_mix_post_0, w_in_0, w_out_0, ln_mlp_pre_0, ln_mlp_post_0, w_up_0, w_down_0,
           ln_mix_pre_1, ln_mix_post_1, w_in_1, w_out_1, ln_mlp_pre_1, ln_mlp_post_1, w_up_1, w_down_1):
    batch, seq, d = x.shape
    row = lambda g: g.reshape(1, d).astype(F32)

    n_qk0 = 2 * RET_HEADS * RET_DK
    w_in_0p = jnp.concatenate(
        [_deinterleave_heads(w_in_0[:, :n_qk0], 2 * RET_HEADS, RET_DK), w_in_0[:, n_qk0:]], axis=1).astype(BF16)
    n_qk1 = 2 * MOBA_HEADS * MOBA_DH
    w_in_1p = jnp.concatenate(
        [_spread_rope_halves(w_in_1[:, :n_qk1], 2 * MOBA_HEADS, MOBA_DH), w_in_1[:, n_qk1:]], axis=1).astype(BF16)
    log_gamma = jnp.log(1.0 - 2.0 ** (-5.0 - jnp.arange(RET_HEADS, dtype=F32)))

    h = x.reshape(batch * seq, d)

    qkvg = _inproj(h, row(ln_mix_pre_0), w_in_0p, _ret_tables(seq), mode="ret", seq=seq)
    mix = _retention(qkvg.reshape(batch, seq, -1), log_gamma, batch, seq)
    h = _outproj_mlp(mix.reshape(batch * seq, -1), h, w_out_0.astype(BF16), row(ln_mix_post_0),
                     row(ln_mlp_pre_0), row(ln_mlp_post_0), w_up_0.astype(BF16), w_down_0.astype(BF16))

    qkv = _inproj(h, row(ln_mix_pre_1), w_in_1p, _moba_tables(seq), mode="moba", seq=seq)
    mix = _moba(qkv.reshape(batch, seq, -1), batch, seq)
    h = _outproj_mlp(mix.reshape(batch * seq, -1), h, w_out_1.astype(BF16), row(ln_mix_post_1),
                     row(ln_mlp_pre_1), row(ln_mlp_post_1), w_up_1.astype(BF16), w_down_1.astype(BF16))
    return h.reshape(batch, seq, d)
```

```python
import functools
import math

import jax
import jax.numpy as jnp
from jax import lax
from jax.experimental import pallas as pl
from jax.experimental.pallas import tpu as pltpu

D_MODEL = 1024
D_FF = 4 * D_MODEL
NORM_EPS = 1e-6
NEG = -1e30
LOG2E = math.log2(math.e)

RET_HEADS = 4
RET_DK = 256
RET_DV = 512
RET_THETA = 10000.0
RET_CHUNK = 256

MOBA_HEADS = 8
MOBA_DH = 128
MOBA_BLOCK = 256
MOBA_TOPK = 3
ROPE_THETA = 500000.0
ROPE_DIM = 32

LANES = 128
BF16_SUBLANES = 16
VMEM_LIMIT = 56 * 1024 * 1024

F32 = jnp.float32
BF16 = jnp.bfloat16
NT_DIMS = (((1,), (1,)), ((), ()))


def _rms_norm(x, g):
    return x * lax.rsqrt(jnp.mean(x * x, axis=-1, keepdims=True) + NORM_EPS) * g


def _resident(shape):
    return pl.BlockSpec(shape, lambda *_: (0,) * len(shape), pipeline_mode=pl.Buffered(1))


def _inproj_outputs(mode):
    if mode == "ret":
        return [(2 * RET_HEADS, RET_DK), (2 * RET_HEADS, RET_DV)]
    return [(3 * MOBA_HEADS, MOBA_DH)]


def _inproj_plan(mode):
    if mode == "ret":
        hq = RET_HEADS * RET_DK
        plan = [(h * RET_DK, RET_DK, 1.0, 0, h) for h in range(RET_HEADS)]
        plan += [(hq + h * RET_DK, RET_DK, RET_DK ** -0.5, 0, RET_HEADS + h) for h in range(RET_HEADS)]
        plan += [(2 * hq + c * RET_DV, RET_DV, None, 1, c) for c in range(2 * RET_HEADS)]
    else:
        hq = MOBA_HEADS * MOBA_DH
        per = 4
        cw = per * MOBA_DH
        plan = [(c * cw, cw, MOBA_DH ** -0.5 * LOG2E, 0, c * per) for c in range(hq // cw)]
        plan += [(hq + c * cw, cw, 1.0, 0, MOBA_HEADS + c * per) for c in range(hq // cw)]
        plan += [(2 * hq + c * cw, cw, None, 0, 2 * MOBA_HEADS + c * per) for c in range(hq // cw)]
    return plan


def _inproj_kernel(x_ref, g_ref, w_ref, cos_ref, sin_ref, *rest, mode):
    out_refs, xs_ref = rest[:-1], rest[-1]
    xs_ref[...] = _rms_norm(x_ref[...], g_ref[...]).astype(BF16)
    cos, sin = cos_ref[...], sin_ref[...]
    scaled = {}
    for start, width, scale, oi, slab0 in _inproj_plan(mode):
        o_ref = out_refs[oi]
        sw = o_ref.shape[-1]
        acc = jnp.dot(xs_ref[...], w_ref[:, start:start + width], preferred_element_type=F32)
        if scale is None:
            for i in range(width // sw):
                o_ref[0, slab0 + i] = acc[:, i * sw:(i + 1) * sw].astype(o_ref.dtype)
            continue
        if scale not in scaled:
            scaled[scale] = (cos * scale, sin * scale)
        c, s = scaled[scale]
        if mode == "ret":
            x1, x2 = acc[:, :LANES], acc[:, LANES:]
            o_ref[0, slab0, :, 0:LANES] = (x1 * c - x2 * s).astype(o_ref.dtype)
            o_ref[0, slab0, :, LANES:width] = (x1 * s + x2 * c).astype(o_ref.dtype)
        else:
            for i in range(width // sw):
                xh = acc[:, i * sw:(i + 1) * sw]
                y = xh * c + pltpu.roll(xh, LANES // 2, 1) * s
                o_ref[0, slab0 + i] = y.astype(o_ref.dtype)


def _inproj(x2d, gain, w, tabs, *, mode, batch, seq, tm=512):
    m_total, d = x2d.shape
    n_total = w.shape[1]
    tiles_per_seq = seq // tm
    tab_spec = pl.BlockSpec((tm, LANES), lambda m: (m % tiles_per_seq, 0))
    outs = _inproj_outputs(mode)
    return pl.pallas_call(
        functools.partial(_inproj_kernel, mode=mode),
        grid=(m_total // tm,),
        in_specs=[pl.BlockSpec((tm, d), lambda m: (m, 0)), _resident((1, d)), _resident((d, n_total)),
                  tab_spec, tab_spec],
        out_specs=[pl.BlockSpec((1, ns, tm, sw), lambda m: (m // tiles_per_seq, 0, m % tiles_per_seq, 0))
                   for ns, sw in outs],
        out_shape=[jax.ShapeDtypeStruct((batch, ns, seq, sw), BF16) for ns, sw in outs],
        scratch_shapes=[pltpu.VMEM((tm, d), BF16)],
        compiler_params=pltpu.CompilerParams(
            dimension_semantics=("arbitrary",), vmem_limit_bytes=VMEM_LIMIT),
        name=f"inproj_{mode}",
    )(x2d, gain, w, *tabs)


def _retention_kernel(lg_ref, q_ref, k_ref, v_ref, g_ref, o_ref, state_ref):
    L = RET_CHUNK
    seq = q_ref.shape[0]
    lg = lg_ref[pl.program_id(1)]

    ii = lax.broadcasted_iota(jnp.int32, (L, L), 0)
    jj = lax.broadcasted_iota(jnp.int32, (L, L), 1)
    diff = (ii - jj).astype(F32)
    d_intra = jnp.where(diff >= 0, jnp.exp(lg * jnp.maximum(diff, 0.0)), 0.0)
    col = lax.broadcasted_iota(jnp.int32, (L, 1), 0).astype(F32)
    q_decay = jnp.exp(lg * (col + 1.0))
    row = lax.broadcasted_iota(jnp.int32, (1, L), 1).astype(F32)
    k_decay = jnp.exp(lg * (L - 1.0 - row))
    chunk_decay = jnp.exp(jnp.full((1, 1), lg * L, F32))

    state_ref[...] = jnp.zeros_like(state_ref)
    for c in range(seq // L):
        rows = pl.ds(c * L, L)
        qc = q_ref[rows, :]
        kc = k_ref[rows, :]
        vc = v_ref[rows, :]
        scores = lax.dot_general(qc, kc, NT_DIMS, preferred_element_type=F32) * d_intra
        o = jnp.dot(scores.astype(BF16), vc, preferred_element_type=F32)
        state = state_ref[...]
        o = o + jnp.dot(qc, state.astype(BF16), preferred_element_type=F32) * q_decay
        kd_t = (kc.astype(F32).T * k_decay).astype(BF16)
        state_ref[...] = state * chunk_decay + jnp.dot(kd_t, vc, preferred_element_type=F32)
        o = o * lax.rsqrt(jnp.mean(o * o, axis=-1, keepdims=True) + NORM_EPS)
        g = g_ref[rows, :].astype(F32)
        o_ref[rows, :] = (o * (g * jax.nn.sigmoid(g))).astype(o_ref.dtype)


def _retention(qk, vg, log_gamma, batch, seq):
    H, DK, DV = RET_HEADS, RET_DK, RET_DV
    grid_spec = pltpu.PrefetchScalarGridSpec(
        num_scalar_prefetch=1,
        grid=(batch, H),
        in_specs=[
            pl.BlockSpec((None, None, seq, DK), lambda b, h, lg: (b, h, 0, 0)),
            pl.BlockSpec((None, None, seq, DK), lambda b, h, lg: (b, H + h, 0, 0)),
            pl.BlockSpec((None, None, seq, DV), lambda b, h, lg: (b, h, 0, 0)),
            pl.BlockSpec((None, None, seq, DV), lambda b, h, lg: (b, H + h, 0, 0)),
        ],
        out_specs=pl.BlockSpec((None, seq, DV), lambda b, h, lg: (b, 0, h)),
        scratch_shapes=[pltpu.VMEM((DK, DV), F32)],
    )
    return pl.pallas_call(
        _retention_kernel,
        grid_spec=grid_spec,
        out_shape=jax.ShapeDtypeStruct((batch, seq, H * DV), BF16),
        compiler_params=pltpu.CompilerParams(
            dimension_semantics=("arbitrary", "arbitrary"), vmem_limit_bytes=VMEM_LIMIT),
        name="retention",
    )(log_gamma, qk, qk, vg, vg)


def _moba_kernel(q_ref, k_ref, v_ref, o_ref, vt_ref, s0_ref, s1_ref, p0_ref, p1_ref):
    s_bufs, p_bufs = (s0_ref, s1_ref), (p0_ref, p1_ref)
    BS = MOBA_BLOCK
    seq = q_ref.shape[0]
    nb = seq // BS

    QT = 2
    order = list(range(nb // QT - 1, -1, -1))

    def scores(i):
        t = order[i]
        nk = (t + 1) * QT * BS
        s_bufs[i % 2][0:nk, :] = lax.dot_general(k_ref[0:nk, :], q_ref[pl.ds(t * QT * BS, QT * BS), :],
                                                 NT_DIMS, preferred_element_type=F32)

    scores(0)

    dh = v_ref.shape[1]
    for j in range(nb):
        cols = pl.ds(j * BS, BS)
        vt_ref[0:dh, cols] = v_ref[cols, :].astype(F32).T.astype(BF16)
    vt_ref[dh:, :] = jnp.ones((vt_ref.shape[0] - dh, seq), BF16)

    gate_floor = NEG * BS * LOG2E
    key_idx = lax.broadcasted_iota(jnp.int32, (BS, BS), 0)
    qry_idx = lax.broadcasted_iota(jnp.int32, (BS, BS), 1)
    causal = key_idx <= qry_idx

    for i, t in enumerate(order):
        nk = (t + 1) * QT * BS
        if i + 1 < len(order):
            scores(i + 1)
        s_cur, p_cur = s_bufs[i % 2], p_bufs[i % 2]

        for hf in range(QT):
            n = t * QT + hf
            cs = pl.ds(hf * BS, BS)

            cmax, gate = [], []
            for j in range(n):
                blk = s_cur[pl.ds(j * BS, BS), cs]
                cmax.append(jnp.max(blk, axis=0, keepdims=True))
                gate.append(jnp.sum(blk, axis=0, keepdims=True))

            sels = []
            for j in range(n):
                rank = jnp.where(gate[j] < gate_floor, float(nb - n), 0.0)
                for i in range(n):
                    if i != j:
                        ahead = (gate[i] >= gate[j]) if i < j else (gate[i] > gate[j])
                        rank = rank + ahead.astype(F32)
                sels.append(rank < MOBA_TOPK)

            s_own = jnp.where(causal, s_cur[pl.ds(n * BS, BS), cs], NEG)
            m = jnp.max(s_own, axis=0, keepdims=True)
            for j in range(n):
                m = jnp.maximum(m, jnp.where(sels[j], cmax[j], NEG))

            p_cur[pl.ds(n * BS, BS), cs] = jnp.exp2(s_own - m).astype(BF16)
            for j in range(n):
                p = jnp.exp2(s_cur[pl.ds(j * BS, BS), cs] - jnp.where(sels[j], m, -NEG))
                p_cur[pl.ds(j * BS, BS), cs] = p.astype(BF16)
            for j in range(n + 1, (t + 1) * QT):
                p_cur[pl.ds(j * BS, BS), cs] = jnp.zeros((BS, BS), BF16)

        acc = jnp.dot(vt_ref[:, 0:nk], p_cur[0:nk, :], preferred_element_type=F32)
        for hf in range(QT):
            cs = slice(hf * BS, (hf + 1) * BS)
            out = acc[0:dh, cs] * (1.0 / acc[dh:dh + 1, cs])
            o_ref[pl.ds((t * QT + hf) * BS, BS), :] = out.T.astype(o_ref.dtype)


def _moba(qkv, batch, seq):
    H, DH = MOBA_HEADS, MOBA_DH
    return pl.pallas_call(
        _moba_kernel,
        grid=(batch, H),
        in_specs=[
            pl.BlockSpec((None, None, seq, DH), lambda b, h: (b, h, 0, 0)),
            pl.BlockSpec((None, None, seq, DH), lambda b, h: (b, H + h, 0, 0)),
            pl.BlockSpec((None, None, seq, DH), lambda b, h: (b, 2 * H + h, 0, 0)),
        ],
        out_specs=pl.BlockSpec((None, seq, DH), lambda b, h: (b, 0, h)),
        out_shape=jax.ShapeDtypeStruct((batch, seq, H * DH), BF16),
        scratch_shapes=[pltpu.VMEM((DH + BF16_SUBLANES, seq), BF16),
                        pltpu.VMEM((seq, 2 * MOBA_BLOCK), F32), pltpu.VMEM((seq, 2 * MOBA_BLOCK), F32),
                        pltpu.VMEM((seq, 2 * MOBA_BLOCK), BF16), pltpu.VMEM((seq, 2 * MOBA_BLOCK), BF16)],
        compiler_params=pltpu.CompilerParams(
            dimension_semantics=("arbitrary", "arbitrary"), vmem_limit_bytes=VMEM_LIMIT),
        name="moba",
    )(qkv, qkv, qkv)


def _outproj_mlp_kernel(a_ref, h_ref, wo_ref, gpost_ref, gpre_ref, gmpost_ref, wup_ref, wdn_ref,
                        o_ref, xn_ref, hid_ref, *, sub, ff_chunk):
    subs = [pl.ds(r, sub) for r in range(0, a_ref.shape[0], sub)]
    for rows in subs:
        mix = jnp.dot(a_ref[rows, :], wo_ref[...], preferred_element_type=F32)
        h1 = h_ref[rows, :] + _rms_norm(mix, gpost_ref[...])
        o_ref[rows, :] = h1
        xn_ref[rows, :] = _rms_norm(h1, gpre_ref[...]).astype(BF16)
    for rows in subs:
        for j in range(wup_ref.shape[1] // ff_chunk):
            cols = pl.ds(j * ff_chunk, ff_chunk)
            u = jnp.maximum(jnp.dot(xn_ref[rows, :], wup_ref[:, cols], preferred_element_type=F32), 0.0)
            hid_ref[rows, cols] = (u * u).astype(BF16)
    for rows in subs:
        dn = jnp.dot(hid_ref[rows, :], wdn_ref[...], preferred_element_type=F32)
        o_ref[rows, :] = o_ref[rows, :] + _rms_norm(dn, gmpost_ref[...])


def _outproj_mlp(a2d, h2d, w_out, g_post, g_mpre, g_mpost, w_up, w_down, *, tm=512, sub=256, ff_chunk=1024):
    m_total, d = h2d.shape
    ka = a2d.shape[1]
    dff = w_up.shape[1]
    return pl.pallas_call(
        functools.partial(_outproj_mlp_kernel, sub=sub, ff_chunk=ff_chunk),
        grid=(m_total // tm,),
        in_specs=[
            pl.BlockSpec((tm, ka), lambda m: (m, 0)),
            pl.BlockSpec((tm, d), lambda m: (m, 0)),
            _resident((ka, d)),
            _resident((1, d)),
            _resident((1, d)),
            _resident((1, d)),
            _resident((d, dff)),
            _resident((dff, d)),
        ],
        out_specs=pl.BlockSpec((tm, d), lambda m: (m, 0)),
        out_shape=jax.ShapeDtypeStruct((m_total, d), F32),
        scratch_shapes=[pltpu.VMEM((tm, d), BF16), pltpu.VMEM((tm, dff), BF16)],
        compiler_params=pltpu.CompilerParams(
            dimension_semantics=("arbitrary",), vmem_limit_bytes=VMEM_LIMIT),
        name=f"outproj_mlp_k{ka}",
    )(a2d, h2d, w_out, g_post, g_mpre, g_mpost, w_up, w_down)


def _ret_tables(seq):
    freq = 1.0 / (RET_THETA ** jnp.linspace(0.0, 1.0, RET_DK // 2, dtype=F32))
    ang = jnp.arange(seq)[:, None].astype(F32) * freq[None, :]
    return jnp.cos(ang), jnp.sin(ang)


def _moba_tables(seq):
    half = ROPE_DIM // 2
    inv = ROPE_THETA ** (-jnp.arange(0, ROPE_DIM, 2, dtype=F32) / ROPE_DIM)
    ang = jnp.arange(seq)[:, None].astype(F32) * inv[None, :]
    cos, sin = jnp.cos(ang), jnp.sin(ang)
    gap = LANES // 2 - half
    c = jnp.concatenate([cos, jnp.ones((seq, gap), F32), cos, jnp.ones((seq, gap), F32)], axis=1)
    s = jnp.concatenate([-sin, jnp.zeros((seq, gap), F32), sin, jnp.zeros((seq, gap), F32)], axis=1)
    return c, s


def _deinterleave_heads(w, n_heads, dk):
    d = w.shape[0]
    return w.reshape(d, n_heads, dk // 2, 2).transpose(0, 1, 3, 2).reshape(d, n_heads * dk)


def _spread_rope_halves(w, n_heads, dh):
    d = w.shape[0]
    half = ROPE_DIM // 2
    gap = dh // 2 - half
    w = w.reshape(d, n_heads, dh)
    x1, x2, rest = w[..., :half], w[..., half:ROPE_DIM], w[..., ROPE_DIM:]
    return jnp.concatenate([x1, rest[..., :gap], x2, rest[..., gap:]], axis=-1).reshape(d, n_heads * dh)


def kernel(x, ln_mix_pre_0, ln_mix_post_0, w_in_0, w_out_0, ln_mlp_pre_0, ln_mlp_post_0, w_up_0, w_down_0,
           ln_mix_pre_1, ln_mix_post_1, w_in_1, w_out_1, ln_mlp_pre_1, ln_mlp_post_1, w_up_1, w_down_1):
    batch, seq, d = x.shape
    row = lambda g: g.reshape(1, d).astype(F32)

    n_qk0 = 2 * RET_HEADS * RET_DK
    w_in_0p = jnp.concatenate(
        [_deinterleave_heads(w_in_0[:, :n_qk0], 2 * RET_HEADS, RET_DK), w_in_0[:, n_qk0:]], axis=1).astype(BF16)
    n_qk1 = 2 * MOBA_HEADS * MOBA_DH
    w_in_1p = jnp.concatenate(
        [_spread_rope_halves(w_in_1[:, :n_qk1], 2 * MOBA_HEADS, MOBA_DH), w_in_1[:, n_qk1:]], axis=1).astype(BF16)
    log_gamma = jnp.log(1.0 - 2.0 ** (-5.0 - jnp.arange(RET_HEADS, dtype=F32)))

    h = x.reshape(batch * seq, d)

    qk, vg = _inproj(h, row(ln_mix_pre_0), w_in_0p, _ret_tables(seq), mode="ret", batch=batch, seq=seq)
    mix = _retention(qk, vg, log_gamma, batch, seq)
    h = _outproj_mlp(mix.reshape(batch * seq, -1), h, w_out_0.astype(BF16), row(ln_mix_post_0),
                     row(ln_mlp_pre_0), row(ln_mlp_post_0), w_up_0.astype(BF16), w_down_0.astype(BF16))

    (qkv,) = _inproj(h, row(ln_mix_pre_1), w_in_1p, _moba_tables(seq), mode="moba", batch=batch, seq=seq)
    mix = _moba(qkv, batch, seq)
    h = _outproj_mlp(mix.reshape(batch * seq, -1), h, w_out_1.astype(BF16), row(ln_mix_post_1),
                     row(ln_mlp_pre_1), row(ln_mlp_post_1), w_up_1.astype(BF16), w_down_1.astype(BF16))
    return h.reshape(batch, seq, d)
```

```python
import functools
import math

import jax
import jax.numpy as jnp
from jax import lax
from jax.experimental import pallas as pl
from jax.experimental.pallas import tpu as pltpu

D_MODEL = 1024
D_FF = 4 * D_MODEL
NORM_EPS = 1e-6
NEG = -1e30
LOG2E = math.log2(math.e)

RET_HEADS = 4
RET_DK = 256
RET_DV = 512
RET_THETA = 10000.0
RET_CHUNK = 256

MOBA_HEADS = 8
MOBA_DH = 128
MOBA_BLOCK = 256
MOBA_TOPK = 3
ROPE_THETA = 500000.0
ROPE_DIM = 32

LANES = 128
BF16_SUBLANES = 16
VMEM_LIMIT = 56 * 1024 * 1024

F32 = jnp.float32
BF16 = jnp.bfloat16
NT_DIMS = (((1,), (1,)), ((), ()))


def _rms_norm(x, g):
    return x * lax.rsqrt(jnp.mean(x * x, axis=-1, keepdims=True) + NORM_EPS) * g


def _resident(shape):
    return pl.BlockSpec(shape, lambda *_: (0,) * len(shape), pipeline_mode=pl.Buffered(1))


def _inproj_outputs(mode):
    if mode == "ret":
        return [(2 * RET_HEADS, RET_DK), (2 * RET_HEADS, RET_DV)]
    return [(3 * MOBA_HEADS, MOBA_DH)]


def _inproj_plan(mode):
    if mode == "ret":
        hq = RET_HEADS * RET_DK
        hv = RET_HEADS * RET_DV
        plan = [(2 * hq + hv + c * RET_DV, RET_DV, "swish", 1, RET_HEADS + c) for c in range(RET_HEADS)]
        plan += [(h * RET_DK, RET_DK, 1.0, 0, h) for h in range(RET_HEADS)]
        plan += [(hq + h * RET_DK, RET_DK, RET_DK ** -0.5, 0, RET_HEADS + h) for h in range(RET_HEADS)]
        plan += [(2 * hq + c * RET_DV, RET_DV, None, 1, c) for c in range(RET_HEADS)]
    else:
        hq = MOBA_HEADS * MOBA_DH
        per = 4
        cw = per * MOBA_DH
        plan = [(c * cw, cw, MOBA_DH ** -0.5 * LOG2E, 0, c * per) for c in range(hq // cw)]
        plan += [(hq + c * cw, cw, 1.0, 0, MOBA_HEADS + c * per) for c in range(hq // cw)]
        plan += [(2 * hq + c * cw, cw, None, 0, 2 * MOBA_HEADS + c * per) for c in range(hq // cw)]
    return plan


def _inproj_kernel(x_ref, g_ref, w_ref, cos_ref, sin_ref, *rest, mode):
    out_refs, xs_ref = rest[:-1], rest[-1]
    xs_ref[...] = _rms_norm(x_ref[...], g_ref[...]).astype(BF16)
    cos, sin = cos_ref[...], sin_ref[...]
    scaled = {}
    for start, width, scale, oi, slab0 in _inproj_plan(mode):
        o_ref = out_refs[oi]
        sw = o_ref.shape[-1]
        acc = jnp.dot(xs_ref[...], w_ref[:, start:start + width], preferred_element_type=F32)
        if scale is None or scale == "swish":
            if scale == "swish":
                acc = acc * jax.nn.sigmoid(acc)
            for i in range(width // sw):
                o_ref[0, slab0 + i] = acc[:, i * sw:(i + 1) * sw].astype(o_ref.dtype)
            continue
        if scale not in scaled:
            scaled[scale] = (cos * scale, sin * scale)
        c, s = scaled[scale]
        if mode == "ret":
            x1, x2 = acc[:, :LANES], acc[:, LANES:]
            o_ref[0, slab0, :, 0:LANES] = (x1 * c - x2 * s).astype(o_ref.dtype)
            o_ref[0, slab0, :, LANES:width] = (x1 * s + x2 * c).astype(o_ref.dtype)
        else:
            for i in range(width // sw):
                xh = acc[:, i * sw:(i + 1) * sw]
                y = xh * c + pltpu.roll(xh, LANES // 2, 1) * s
                o_ref[0, slab0 + i] = y.astype(o_ref.dtype)


def _inproj(x2d, gain, w, tabs, *, mode, batch, seq, tm=512):
    m_total, d = x2d.shape
    n_total = w.shape[1]
    tiles_per_seq = seq // tm
    tab_spec = pl.BlockSpec((tm, LANES), lambda m: (m % tiles_per_seq, 0))
    outs = _inproj_outputs(mode)
    return pl.pallas_call(
        functools.partial(_inproj_kernel, mode=mode),
        grid=(m_total // tm,),
        in_specs=[pl.BlockSpec((tm, d), lambda m: (m, 0)), _resident((1, d)), _resident((d, n_total)),
                  tab_spec, tab_spec],
        out_specs=[pl.BlockSpec((1, ns, tm, sw), lambda m: (m // tiles_per_seq, 0, m % tiles_per_seq, 0))
                   for ns, sw in outs],
        out_shape=[jax.ShapeDtypeStruct((batch, ns, seq, sw), BF16) for ns, sw in outs],
        scratch_shapes=[pltpu.VMEM((tm, d), BF16)],
        compiler_params=pltpu.CompilerParams(
            dimension_semantics=("arbitrary",), vmem_limit_bytes=VMEM_LIMIT),
        name=f"inproj_{mode}",
    )(x2d, gain, w, *tabs)


def _retention_kernel(lg_ref, q_ref, k_ref, v_ref, g_ref, o_ref, state_ref):
    L = RET_CHUNK
    seq = q_ref.shape[0]
    lg = lg_ref[pl.program_id(1)]

    ii = lax.broadcasted_iota(jnp.int32, (L, L), 0)
    jj = lax.broadcasted_iota(jnp.int32, (L, L), 1)
    diff = (ii - jj).astype(F32)
    d_intra = jnp.where(diff >= 0, jnp.exp(lg * jnp.maximum(diff, 0.0)), 0.0)
    col = lax.broadcasted_iota(jnp.int32, (L, 1), 0).astype(F32)
    q_decay = jnp.exp(lg * (col + 1.0))
    row = lax.broadcasted_iota(jnp.int32, (1, L), 1).astype(F32)
    k_decay = jnp.exp(lg * (L - 1.0 - row))
    chunk_decay = jnp.exp(jnp.full((1, 1), lg * L, F32))

    state_ref[...] = jnp.zeros_like(state_ref)
    for c in range(seq // L):
        rows = pl.ds(c * L, L)
        qc = q_ref[rows, :]
        kc = k_ref[rows, :]
        vc = v_ref[rows, :]
        scores = lax.dot_general(qc, kc, NT_DIMS, preferred_element_type=F32) * d_intra
        o = jnp.dot(scores.astype(BF16), vc, preferred_element_type=F32)
        state = state_ref[...]
        o = o + jnp.dot(qc, state.astype(BF16), preferred_element_type=F32) * q_decay
        kd_t = (kc.astype(F32).T * k_decay).astype(BF16)
        state_ref[...] = state * chunk_decay + jnp.dot(kd_t, vc, preferred_element_type=F32)
        o = o * lax.rsqrt(jnp.mean(o * o, axis=-1, keepdims=True) + NORM_EPS)
        o_ref[rows, :] = (o * g_ref[rows, :].astype(F32)).astype(o_ref.dtype)


def _retention(qk, vg, log_gamma, batch, seq):
    H, DK, DV = RET_HEADS, RET_DK, RET_DV
    grid_spec = pltpu.PrefetchScalarGridSpec(
        num_scalar_prefetch=1,
        grid=(batch, H),
        in_specs=[
            pl.BlockSpec((None, None, seq, DK), lambda b, h, lg: (b, h, 0, 0)),
            pl.BlockSpec((None, None, seq, DK), lambda b, h, lg: (b, H + h, 0, 0)),
            pl.BlockSpec((None, None, seq, DV), lambda b, h, lg: (b, h, 0, 0)),
            pl.BlockSpec((None, None, seq, DV), lambda b, h, lg: (b, H + h, 0, 0)),
        ],
        out_specs=pl.BlockSpec((None, seq, DV), lambda b, h, lg: (b, 0, h)),
        scratch_shapes=[pltpu.VMEM((DK, DV), F32)],
    )
    return pl.pallas_call(
        _retention_kernel,
        grid_spec=grid_spec,
        out_shape=jax.ShapeDtypeStruct((batch, seq, H * DV), BF16),
        compiler_params=pltpu.CompilerParams(
            dimension_semantics=("arbitrary", "arbitrary"), vmem_limit_bytes=VMEM_LIMIT),
        name="retention",
    )(log_gamma, qk, qk, vg, vg)


def _moba_kernel(q_ref, k_ref, v_ref, o_ref, ka_ref, ksum_ref, vt_ref, s0_ref, s1_ref, p0_ref, p1_ref):
    s_bufs, p_bufs = (s0_ref, s1_ref), (p0_ref, p1_ref)
    BS = MOBA_BLOCK
    seq, dh = q_ref.shape
    nb = seq // BS
    G = ka_ref.shape[0] - seq

    for j in range(nb):
        ksum = jnp.sum(k_ref[pl.ds(j * BS, BS), :].astype(F32), axis=0, keepdims=True)
        hi = ksum.astype(BF16).astype(F32)
        ksum_ref[j:j + 1, :] = hi
        ksum_ref[nb + j:nb + j + 1, :] = ksum - hi
    ka_ref[0:G, :] = ksum_ref[...].astype(BF16)
    ka_ref[G:, :] = k_ref[...]
    gate_floor = NEG * BS * LOG2E

    QT = 2
    order = list(range(nb // QT - 1, -1, -1))

    def scores(i):
        t = order[i]
        nr = G + (t + 1) * QT * BS
        s_bufs[i % 2][0:nr, :] = lax.dot_general(ka_ref[0:nr, :], q_ref[pl.ds(t * QT * BS, QT * BS), :],
                                                 NT_DIMS, preferred_element_type=F32)

    scores(0)

    for j in range(nb):
        cols = pl.ds(j * BS, BS)
        vt_ref[0:dh, cols] = v_ref[cols, :].astype(F32).T.astype(BF16)
    vt_ref[dh:, :] = jnp.ones((vt_ref.shape[0] - dh, seq), BF16)

    key_idx = lax.broadcasted_iota(jnp.int32, (BS, BS), 0)
    qry_idx = lax.broadcasted_iota(jnp.int32, (BS, BS), 1)
    causal = key_idx <= qry_idx

    for i, t in enumerate(order):
        nk = (t + 1) * QT * BS
        if i + 1 < len(order):
            scores(i + 1)
        s_cur, p_cur = s_bufs[i % 2], p_bufs[i % 2]

        for hf in range(QT):
            n = t * QT + hf
            cs = pl.ds(hf * BS, BS)

            gate = [s_cur[j:j + 1, cs] + s_cur[nb + j:nb + j + 1, cs] for j in range(n)]

            sels = []
            for j in range(n):
                rank = jnp.where(gate[j] < gate_floor, float(nb - n), 0.0)
                for r in range(n):
                    if r != j:
                        ahead = (gate[r] >= gate[j]) if r < j else (gate[r] > gate[j])
                        rank = rank + ahead.astype(F32)
                sels.append(rank < MOBA_TOPK)

            s_own = jnp.where(causal, s_cur[pl.ds(G + n * BS, BS), cs], NEG)
            m = jnp.max(s_own, axis=0, keepdims=True)
            for j in range(n):
                m_blk = jnp.max(s_cur[pl.ds(G + j * BS, BS), cs], axis=0, keepdims=True)
                m = jnp.maximum(m, jnp.where(sels[j], m_blk, NEG))

            p_cur[pl.ds(n * BS, BS), cs] = jnp.exp2(s_own - m).astype(BF16)
            for j in range(n):
                p = jnp.exp2(s_cur[pl.ds(G + j * BS, BS), cs] - jnp.where(sels[j], m, -NEG))
                p_cur[pl.ds(j * BS, BS), cs] = p.astype(BF16)
            for j in range(n + 1, (t + 1) * QT):
                p_cur[pl.ds(j * BS, BS), cs] = jnp.zeros((BS, BS), BF16)

        acc = jnp.dot(vt_ref[:, 0:nk], p_cur[0:nk, :], preferred_element_type=F32)
        for hf in range(QT):
            cs = slice(hf * BS, (hf + 1) * BS)
            out = acc[0:dh, cs] * (1.0 / acc[dh:dh + 1, cs])
            o_ref[pl.ds((t * QT + hf) * BS, BS), :] = out.T.astype(o_ref.dtype)


def _moba(qkv, batch, seq):
    H, DH = MOBA_HEADS, MOBA_DH
    assert 2 * (seq // MOBA_BLOCK) == BF16_SUBLANES, "hi and lo key-sum rows must fill one bf16 tile"
    return pl.pallas_call(
        _moba_kernel,
        grid=(batch, H),
        in_specs=[
            pl.BlockSpec((None, None, seq, DH), lambda b, h: (b, h, 0, 0)),
            pl.BlockSpec((None, None, seq, DH), lambda b, h: (b, H + h, 0, 0)),
            pl.BlockSpec((None, None, seq, DH), lambda b, h: (b, 2 * H + h, 0, 0)),
        ],
        out_specs=pl.BlockSpec((None, seq, DH), lambda b, h: (b, 0, h)),
        out_shape=jax.ShapeDtypeStruct((batch, seq, H * DH), BF16),
        scratch_shapes=[pltpu.VMEM((BF16_SUBLANES + seq, DH), BF16),
                        pltpu.VMEM((BF16_SUBLANES, DH), F32),
                        pltpu.VMEM((DH + BF16_SUBLANES, seq), BF16),
                        pltpu.VMEM((BF16_SUBLANES + seq, 2 * MOBA_BLOCK), F32),
                        pltpu.VMEM((BF16_SUBLANES + seq, 2 * MOBA_BLOCK), F32),
                        pltpu.VMEM((seq, 2 * MOBA_BLOCK), BF16), pltpu.VMEM((seq, 2 * MOBA_BLOCK), BF16)],
        compiler_params=pltpu.CompilerParams(
            dimension_semantics=("arbitrary", "arbitrary"), vmem_limit_bytes=VMEM_LIMIT),
        name="moba",
    )(qkv, qkv, qkv)


def _outproj_mlp_kernel(a_ref, h_ref, wo_ref, gpost_ref, gpre_ref, gmpost_ref, wup_ref, wdn_ref,
                        o_ref, xn_ref, hid_ref, *, sub, ff_chunk):
    subs = [pl.ds(r, sub) for r in range(0, a_ref.shape[0], sub)]
    for rows in subs:
        mix = jnp.dot(a_ref[rows, :], wo_ref[...], preferred_element_type=F32)
        h1 = h_ref[rows, :] + _rms_norm(mix, gpost_ref[...])
        o_ref[rows, :] = h1
        xn_ref[rows, :] = _rms_norm(h1, gpre_ref[...]).astype(BF16)
    for rows in subs:
        for j in range(wup_ref.shape[1] // ff_chunk):
            cols = pl.ds(j * ff_chunk, ff_chunk)
            u = jnp.maximum(jnp.dot(xn_ref[rows, :], wup_ref[:, cols], preferred_element_type=F32), 0.0)
            hid_ref[rows, cols] = (u * u).astype(BF16)
    for rows in subs:
        dn = jnp.dot(hid_ref[rows, :], wdn_ref[...], preferred_element_type=F32)
        o_ref[rows, :] = o_ref[rows, :] + _rms_norm(dn, gmpost_ref[...])


def _outproj_mlp(a2d, h2d, w_out, g_post, g_mpre, g_mpost, w_up, w_down, *, tm=512, sub=256, ff_chunk=1024):
    m_total, d = h2d.shape
    ka = a2d.shape[1]
    dff = w_up.shape[1]
    return pl.pallas_call(
        functools.partial(_outproj_mlp_kernel, sub=sub, ff_chunk=ff_chunk),
        grid=(m_total // tm,),
        in_specs=[
            pl.BlockSpec((tm, ka), lambda m: (m, 0)),
            pl.BlockSpec((tm, d), lambda m: (m, 0)),
            _resident((ka, d)),
            _resident((1, d)),
            _resident((1, d)),
            _resident((1, d)),
            _resident((d, dff)),
            _resident((dff, d)),
        ],
        out_specs=pl.BlockSpec((tm, d), lambda m: (m, 0)),
        out_shape=jax.ShapeDtypeStruct((m_total, d), F32),
        scratch_shapes=[pltpu.VMEM((tm, d), BF16), pltpu.VMEM((tm, dff), BF16)],
        compiler_params=pltpu.CompilerParams(
            dimension_semantics=("arbitrary",), vmem_limit_bytes=VMEM_LIMIT),
        name=f"outproj_mlp_k{ka}",
    )(a2d, h2d, w_out, g_post, g_mpre, g_mpost, w_up, w_down)


def _ret_tables(seq):
    freq = 1.0 / (RET_THETA ** jnp.linspace(0.0, 1.0, RET_DK // 2, dtype=F32))
    ang = jnp.arange(seq)[:, None].astype(F32) * freq[None, :]
    return jnp.cos(ang), jnp.sin(ang)


def _moba_tables(seq):
    half = ROPE_DIM // 2
    inv = ROPE_THETA ** (-jnp.arange(0, ROPE_DIM, 2, dtype=F32) / ROPE_DIM)
    ang = jnp.arange(seq)[:, None].astype(F32) * inv[None, :]
    cos, sin = jnp.cos(ang), jnp.sin(ang)
    gap = LANES // 2 - half
    c = jnp.concatenate([cos, jnp.ones((seq, gap), F32), cos, jnp.ones((seq, gap), F32)], axis=1)
    s = jnp.concatenate([-sin, jnp.zeros((seq, gap), F32), sin, jnp.zeros((seq, gap), F32)], axis=1)
    return c, s


def _deinterleave_heads(w, n_heads, dk):
    d = w.shape[0]
    return w.reshape(d, n_heads, dk // 2, 2).transpose(0, 1, 3, 2).reshape(d, n_heads * dk)


def _spread_rope_halves(w, n_heads, dh):
    d = w.shape[0]
    half = ROPE_DIM // 2
    gap = dh // 2 - half
    w = w.reshape(d, n_heads, dh)
    x1, x2, rest = w[..., :half], w[..., half:ROPE_DIM], w[..., ROPE_DIM:]
    return jnp.concatenate([x1, rest[..., :gap], x2, rest[..., gap:]], axis=-1).reshape(d, n_heads * dh)


def kernel(x, ln_mix_pre_0, ln_mix_post_0, w_in_0, w_out_0, ln_mlp_pre_0, ln_mlp_post_0, w_up_0, w_down_0,
           ln_mix_pre_1, ln_mix_post_1, w_in_1, w_out_1, ln_mlp_pre_1, ln_mlp_post_1, w_up_1, w_down_1):
    batch, seq, d = x.shape
    row = lambda g: g.reshape(1, d).astype(F32)

    n_qk0 = 2 * RET_HEADS * RET_DK
    w_in_0p = jnp.concatenate(
        [_deinterleave_heads(w_in_0[:, :n_qk0], 2 * RET_HEADS, RET_DK), w_in_0[:, n_qk0:]], axis=1).astype(BF16)
    n_qk1 = 2 * MOBA_HEADS * MOBA_DH
    w_in_1p = jnp.concatenate(
        [_spread_rope_halves(w_in_1[:, :n_qk1], 2 * MOBA_HEADS, MOBA_DH), w_in_1[:, n_qk1:]], axis=1).astype(BF16)
    log_gamma = jnp.log(1.0 - 2.0 ** (-5.0 - jnp.arange(RET_HEADS, dtype=F32)))

    h = x.reshape(batch * seq, d)

    qk, vg = _inproj(h, row(ln_mix_pre_0), w_in_0p, _ret_tables(seq), mode="ret", batch=batch, seq=seq)
    mix = _retention(qk, vg, log_gamma, batch, seq)
    h = _outproj_mlp(mix.reshape(batch * seq, -1), h, w_out_0.astype(BF16), row(ln_mix_post_0),
                     row(ln_mlp_pre_0), row(ln_mlp_post_0), w_up_0.astype(BF16), w_down_0.astype(BF16))

    (qkv,) = _inproj(h, row(ln_mix_pre_1), w_in_1p, _moba_tables(seq), mode="moba", batch=batch, seq=seq)
    mix = _moba(qkv, batch, seq)
    h = _outproj_mlp(mix.reshape(batch * seq, -1), h, w_out_1.astype(BF16), row(ln_mix_post_1),
                     row(ln_mlp_pre_1), row(ln_mlp_post_1), w_up_1.astype(BF16), w_down_1.astype(BF16))
    return h.reshape(batch, seq, d)
```

```python
import functools
import math

import jax
import jax.numpy as jnp
from jax import lax
from jax.experimental import pallas as pl
from jax.experimental.pallas import tpu as pltpu

D_MODEL = 1024
D_FF = 4 * D_MODEL
NORM_EPS = 1e-6
NEG = -1e30
LOG2E = math.log2(math.e)

RET_HEADS = 4
RET_DK = 256
RET_DV = 512
RET_THETA = 10000.0
RET_CHUNK = 256

MOBA_HEADS = 8
MOBA_DH = 128
MOBA_BLOCK = 256
MOBA_TOPK = 3
ROPE_THETA = 500000.0
ROPE_DIM = 32

LANES = 128
BF16_SUBLANES = 16
VMEM_LIMIT = 56 * 1024 * 1024

F32 = jnp.float32
BF16 = jnp.bfloat16
NT_DIMS = (((1,), (1,)), ((), ()))


def _rms_norm(x, g):
    return x * lax.rsqrt(jnp.mean(x * x, axis=-1, keepdims=True) + NORM_EPS) * g


def _resident(shape):
    return pl.BlockSpec(shape, lambda *_: (0,) * len(shape), pipeline_mode=pl.Buffered(1))


def _inproj_outputs(mode):
    if mode == "ret":
        return [(2 * RET_HEADS, RET_DK), (2 * RET_HEADS, RET_DV)]
    return [(3 * MOBA_HEADS, MOBA_DH)]


def _inproj_plan(mode):
    if mode == "ret":
        hq = RET_HEADS * RET_DK
        hv = RET_HEADS * RET_DV
        plan = [(1, hv + c * RET_DV, RET_DV, "swish", 1, RET_HEADS + c) for c in range(RET_HEADS)]
        plan += [(0, h * RET_DK, RET_DK, 1.0, 0, h) for h in range(RET_HEADS)]
        plan += [(0, hq + h * RET_DK, RET_DK, RET_DK ** -0.5, 0, RET_HEADS + h) for h in range(RET_HEADS)]
        plan += [(1, c * RET_DV, RET_DV, None, 1, c) for c in range(RET_HEADS)]
    else:
        hq = MOBA_HEADS * MOBA_DH
        per = 4
        cw = per * MOBA_DH
        plan = [(0, c * cw, cw, MOBA_DH ** -0.5 * LOG2E, 0, c * per) for c in range(hq // cw)]
        plan += [(0, hq + c * cw, cw, 1.0, 0, MOBA_HEADS + c * per) for c in range(hq // cw)]
        plan += [(1, c * cw, cw, None, 0, 2 * MOBA_HEADS + c * per) for c in range(hq // cw)]
    return plan


def _inproj_kernel(x_ref, g_ref, wqk_ref, wrest_ref, cos_ref, sin_ref, *rest, mode):
    out_refs, xs_ref = rest[:-1], rest[-1]
    xs_ref[...] = _rms_norm(x_ref[...], g_ref[...]).astype(BF16)
    cos, sin = cos_ref[...], sin_ref[...]
    scaled = {}
    for wi, start, width, scale, oi, slab0 in _inproj_plan(mode):
        o_ref = out_refs[oi]
        sw = o_ref.shape[-1]
        w_ref = (wqk_ref, wrest_ref)[wi]
        acc = jnp.dot(xs_ref[...], w_ref[:, start:start + width], preferred_element_type=F32)
        if scale is None or scale == "swish":
            if scale == "swish":
                half = 0.5 * acc
                acc = half + half * jnp.tanh(half)
            for i in range(width // sw):
                o_ref[0, slab0 + i] = acc[:, i * sw:(i + 1) * sw].astype(o_ref.dtype)
            continue
        if scale not in scaled:
            scaled[scale] = (cos * scale, sin * scale)
        c, s = scaled[scale]
        if mode == "ret":
            x1, x2 = acc[:, :LANES], acc[:, LANES:]
            o_ref[0, slab0, :, 0:LANES] = (x1 * c - x2 * s).astype(o_ref.dtype)
            o_ref[0, slab0, :, LANES:width] = (x1 * s + x2 * c).astype(o_ref.dtype)
        else:
            for i in range(width // sw):
                xh = acc[:, i * sw:(i + 1) * sw]
                y = xh * c + pltpu.roll(xh, LANES // 2, 1) * s
                o_ref[0, slab0 + i] = y.astype(o_ref.dtype)


def _inproj(x2d, gain, w_qk, w_rest, tabs, *, mode, batch, seq, tm=512):
    m_total, d = x2d.shape
    tiles_per_seq = seq // tm
    tab_spec = pl.BlockSpec((tm, LANES), lambda m: (m % tiles_per_seq, 0))
    outs = _inproj_outputs(mode)
    return pl.pallas_call(
        functools.partial(_inproj_kernel, mode=mode),
        grid=(m_total // tm,),
        in_specs=[pl.BlockSpec((tm, d), lambda m: (m, 0)), _resident((1, d)),
                  _resident(w_qk.shape), _resident(w_rest.shape), tab_spec, tab_spec],
        out_specs=[pl.BlockSpec((1, ns, tm, sw), lambda m: (m // tiles_per_seq, 0, m % tiles_per_seq, 0))
                   for ns, sw in outs],
        out_shape=[jax.ShapeDtypeStruct((batch, ns, seq, sw), BF16) for ns, sw in outs],
        scratch_shapes=[pltpu.VMEM((tm, d), BF16)],
        compiler_params=pltpu.CompilerParams(
            dimension_semantics=("arbitrary",), vmem_limit_bytes=VMEM_LIMIT),
        name=f"inproj_{mode}",
    )(x2d, gain, w_qk, w_rest, *tabs)


def _retention_kernel(lg_ref, q_ref, k_ref, v_ref, g_ref, o_ref, state_ref):
    L = RET_CHUNK
    seq = q_ref.shape[0]
    lg = lg_ref[pl.program_id(1)]

    ii = lax.broadcasted_iota(jnp.int32, (L, L), 0)
    jj = lax.broadcasted_iota(jnp.int32, (L, L), 1)
    diff = (ii - jj).astype(F32)
    d_intra = jnp.where(diff >= 0, jnp.exp(lg * jnp.maximum(diff, 0.0)), 0.0)
    col = lax.broadcasted_iota(jnp.int32, (L, 1), 0).astype(F32)
    q_decay = jnp.exp(lg * (col + 1.0))
    row = lax.broadcasted_iota(jnp.int32, (1, L), 1).astype(F32)
    k_decay = jnp.exp(lg * (L - 1.0 - row))
    chunk_decay = jnp.exp(jnp.full((1, 1), lg * L, F32))

    state_ref[...] = jnp.zeros_like(state_ref)
    for c in range(seq // L):
        rows = pl.ds(c * L, L)
        qc = q_ref[rows, :]
        kc = k_ref[rows, :]
        vc = v_ref[rows, :]
        scores = lax.dot_general(qc, kc, NT_DIMS, preferred_element_type=F32) * d_intra
        o = jnp.dot(scores.astype(BF16), vc, preferred_element_type=F32)
        state = state_ref[...]
        o = o + jnp.dot(qc, state.astype(BF16), preferred_element_type=F32) * q_decay
        kd_t = (kc.astype(F32).T * k_decay).astype(BF16)
        state_ref[...] = state * chunk_decay + jnp.dot(kd_t, vc, preferred_element_type=F32)
        o = o * lax.rsqrt(jnp.mean(o * o, axis=-1, keepdims=True) + NORM_EPS)
        o_ref[rows, :] = (o * g_ref[rows, :].astype(F32)).astype(o_ref.dtype)


def _retention(qk, vg, log_gamma, batch, seq):
    H, DK, DV = RET_HEADS, RET_DK, RET_DV
    grid_spec = pltpu.PrefetchScalarGridSpec(
        num_scalar_prefetch=1,
        grid=(batch, H),
        in_specs=[
            pl.BlockSpec((None, None, seq, DK), lambda b, h, lg: (b, h, 0, 0)),
            pl.BlockSpec((None, None, seq, DK), lambda b, h, lg: (b, H + h, 0, 0)),
            pl.BlockSpec((None, None, seq, DV), lambda b, h, lg: (b, h, 0, 0)),
            pl.BlockSpec((None, None, seq, DV), lambda b, h, lg: (b, H + h, 0, 0)),
        ],
        out_specs=pl.BlockSpec((None, seq, DV), lambda b, h, lg: (b, 0, h)),
        scratch_shapes=[pltpu.VMEM((DK, DV), F32)],
    )
    return pl.pallas_call(
        _retention_kernel,
        grid_spec=grid_spec,
        out_shape=jax.ShapeDtypeStruct((batch, seq, H * DV), BF16),
        compiler_params=pltpu.CompilerParams(
            dimension_semantics=("arbitrary", "arbitrary"), vmem_limit_bytes=VMEM_LIMIT),
        name="retention",
    )(log_gamma, qk, qk, vg, vg)


def _moba_kernel(q_ref, k_ref, v_ref, o_ref, ka_ref, ksum_ref, vt_ref, s0_ref, s1_ref, p0_ref, p1_ref):
    s_bufs, p_bufs = (s0_ref, s1_ref), (p0_ref, p1_ref)
    BS = MOBA_BLOCK
    seq, dh = q_ref.shape
    nb = seq // BS
    G = ka_ref.shape[0] - seq

    for j in range(nb):
        ksum = jnp.sum(k_ref[pl.ds(j * BS, BS), :].astype(F32), axis=0, keepdims=True)
        hi = ksum.astype(BF16).astype(F32)
        ksum_ref[j:j + 1, :] = hi
        ksum_ref[nb + j:nb + j + 1, :] = ksum - hi
    ka_ref[0:G, :] = ksum_ref[...].astype(BF16)
    ka_ref[G:, :] = k_ref[...]
    gate_floor = NEG * BS * LOG2E

    QT = 2
    order = list(range(nb // QT - 1, -1, -1))

    def scores(i):
        t = order[i]
        nr = G + (t + 1) * QT * BS
        s_bufs[i % 2][0:nr, :] = lax.dot_general(ka_ref[0:nr, :], q_ref[pl.ds(t * QT * BS, QT * BS), :],
                                                 NT_DIMS, preferred_element_type=F32)

    scores(0)

    for j in range(nb):
        cols = pl.ds(j * BS, BS)
        vt_ref[0:dh, cols] = v_ref[cols, :].astype(F32).T.astype(BF16)
    vt_ref[dh:, :] = jnp.ones((vt_ref.shape[0] - dh, seq), BF16)

    key_idx = lax.broadcasted_iota(jnp.int32, (BS, BS), 0)
    qry_idx = lax.broadcasted_iota(jnp.int32, (BS, BS), 1)
    causal = key_idx <= qry_idx

    for i, t in enumerate(order):
        nk = (t + 1) * QT * BS
        if i + 1 < len(order):
            scores(i + 1)
        s_cur, p_cur = s_bufs[i % 2], p_bufs[i % 2]

        for hf in range(QT):
            n = t * QT + hf
            cs = pl.ds(hf * BS, BS)

            gate = [s_cur[j:j + 1, cs] + s_cur[nb + j:nb + j + 1, cs] for j in range(n)]

            sels = []
            for j in range(n):
                rank = jnp.where(gate[j] < gate_floor, float(nb - n), 0.0)
                for r in range(n):
                    if r != j:
                        ahead = (gate[r] >= gate[j]) if r < j else (gate[r] > gate[j])
                        rank = rank + ahead.astype(F32)
                sels.append(rank < MOBA_TOPK)

            s_own = jnp.where(causal, s_cur[pl.ds(G + n * BS, BS), cs], NEG)
            m = jnp.max(s_own, axis=0, keepdims=True)
            for j in range(n):
                m_blk = jnp.max(s_cur[pl.ds(G + j * BS, BS), cs], axis=0, keepdims=True)
                m = jnp.maximum(m, jnp.where(sels[j], m_blk, NEG))

            p_cur[pl.ds(n * BS, BS), cs] = jnp.exp2(s_own - m).astype(BF16)
            for j in range(n):
                p = jnp.exp2(s_cur[pl.ds(G + j * BS, BS), cs] - jnp.where(sels[j], m, -NEG))
                p_cur[pl.ds(j * BS, BS), cs] = p.astype(BF16)
            for j in range(n + 1, (t + 1) * QT):
                p_cur[pl.ds(j * BS, BS), cs] = jnp.zeros((BS, BS), BF16)

        acc = jnp.dot(vt_ref[:, 0:nk], p_cur[0:nk, :], preferred_element_type=F32)
        for hf in range(QT):
            cs = slice(hf * BS, (hf + 1) * BS)
            out = acc[0:dh, cs] * (1.0 / acc[dh:dh + 1, cs])
            o_ref[pl.ds((t * QT + hf) * BS, BS), :] = out.T.astype(o_ref.dtype)


def _moba(qkv, batch, seq):
    H, DH = MOBA_HEADS, MOBA_DH
    assert 2 * (seq // MOBA_BLOCK) == BF16_SUBLANES, "hi and lo key-sum rows must fill one bf16 tile"
    return pl.pallas_call(
        _moba_kernel,
        grid=(batch, H),
        in_specs=[
            pl.BlockSpec((None, None, seq, DH), lambda b, h: (b, h, 0, 0)),
            pl.BlockSpec((None, None, seq, DH), lambda b, h: (b, H + h, 0, 0)),
            pl.BlockSpec((None, None, seq, DH), lambda b, h: (b, 2 * H + h, 0, 0)),
        ],
        out_specs=pl.BlockSpec((None, seq, DH), lambda b, h: (b, 0, h)),
        out_shape=jax.ShapeDtypeStruct((batch, seq, H * DH), BF16),
        scratch_shapes=[pltpu.VMEM((BF16_SUBLANES + seq, DH), BF16),
                        pltpu.VMEM((BF16_SUBLANES, DH), F32),
                        pltpu.VMEM((DH + BF16_SUBLANES, seq), BF16),
                        pltpu.VMEM((BF16_SUBLANES + seq, 2 * MOBA_BLOCK), F32),
                        pltpu.VMEM((BF16_SUBLANES + seq, 2 * MOBA_BLOCK), F32),
                        pltpu.VMEM((seq, 2 * MOBA_BLOCK), BF16), pltpu.VMEM((seq, 2 * MOBA_BLOCK), BF16)],
        compiler_params=pltpu.CompilerParams(
            dimension_semantics=("arbitrary", "arbitrary"), vmem_limit_bytes=VMEM_LIMIT),
        name="moba",
    )(qkv, qkv, qkv)


def _outproj_mlp_kernel(a_ref, h_ref, wo_ref, gpost_ref, gpre_ref, gmpost_ref, wup_ref, wdn_ref,
                        o_ref, xn_ref, hid_ref, *, sub, ff_chunk):
    assert sum(sub) == a_ref.shape[0]
    subs = [pl.ds(sum(sub[:i]), sub[i]) for i in range(len(sub))]
    for rows in subs:
        mix = jnp.dot(a_ref[rows, :], wo_ref[...], preferred_element_type=F32)
        h1 = h_ref[rows, :] + _rms_norm(mix, gpost_ref[...])
        o_ref[rows, :] = h1
        xn_ref[rows, :] = _rms_norm(h1, gpre_ref[...]).astype(BF16)
    for rows in subs:
        for j in range(wup_ref.shape[1] // ff_chunk):
            cols = pl.ds(j * ff_chunk, ff_chunk)
            u = jnp.maximum(jnp.dot(xn_ref[rows, :], wup_ref[:, cols], preferred_element_type=F32), 0.0)
            hid_ref[rows, cols] = (u * u).astype(BF16)
    for rows in subs:
        dn = jnp.dot(hid_ref[rows, :], wdn_ref[...], preferred_element_type=F32)
        o_ref[rows, :] = o_ref[rows, :] + _rms_norm(dn, gmpost_ref[...])


def _outproj_mlp(a2d, h2d, w_out, g_post, g_mpre, g_mpost, w_up, w_down, *, tm=512, sub=(256, 256), ff_chunk=1024):
    m_total, d = h2d.shape
    ka = a2d.shape[1]
    dff = w_up.shape[1]
    return pl.pallas_call(
        functools.partial(_outproj_mlp_kernel, sub=sub, ff_chunk=ff_chunk),
        grid=(m_total // tm,),
        in_specs=[
            pl.BlockSpec((tm, ka), lambda m: (m, 0)),
            pl.BlockSpec((tm, d), lambda m: (m, 0)),
            _resident((ka, d)),
            _resident((1, d)),
            _resident((1, d)),
            _resident((1, d)),
            _resident((d, dff)),
            _resident((dff, d)),
        ],
        out_specs=pl.BlockSpec((tm, d), lambda m: (m, 0)),
        out_shape=jax.ShapeDtypeStruct((m_total, d), F32),
        scratch_shapes=[pltpu.VMEM((tm, d), BF16), pltpu.VMEM((tm, dff), BF16)],
        compiler_params=pltpu.CompilerParams(
            dimension_semantics=("arbitrary",), vmem_limit_bytes=VMEM_LIMIT),
        name=f"outproj_mlp_k{ka}",
    )(a2d, h2d, w_out, g_post, g_mpre, g_mpost, w_up, w_down)


def _ret_tables(seq):
    freq = 1.0 / (RET_THETA ** jnp.linspace(0.0, 1.0, RET_DK // 2, dtype=F32))
    ang = jnp.arange(seq)[:, None].astype(F32) * freq[None, :]
    return jnp.cos(ang), jnp.sin(ang)


def _moba_tables(seq):
    half = ROPE_DIM // 2
    inv = ROPE_THETA ** (-jnp.arange(0, ROPE_DIM, 2, dtype=F32) / ROPE_DIM)
    ang = jnp.arange(seq)[:, None].astype(F32) * inv[None, :]
    cos, sin = jnp.cos(ang), jnp.sin(ang)
    gap = LANES // 2 - half
    c = jnp.concatenate([cos, jnp.ones((seq, gap), F32), cos, jnp.ones((seq, gap), F32)], axis=1)
    s = jnp.concatenate([-sin, jnp.zeros((seq, gap), F32), sin, jnp.zeros((seq, gap), F32)], axis=1)
    return c, s


def _deinterleave_heads(w, n_heads, dk):
    d = w.shape[0]
    return w.reshape(d, n_heads, dk // 2, 2).transpose(0, 1, 3, 2).reshape(d, n_heads * dk)


def _spread_rope_halves(w, n_heads, dh):
    d = w.shape[0]
    half = ROPE_DIM // 2
    gap = dh // 2 - half
    w = w.reshape(d, n_heads, dh)
    x1, x2, rest = w[..., :half], w[..., half:ROPE_DIM], w[..., ROPE_DIM:]
    return jnp.concatenate([x1, rest[..., :gap], x2, rest[..., gap:]], axis=-1).reshape(d, n_heads * dh)


def kernel(x, ln_mix_pre_0, ln_mix_post_0, w_in_0, w_out_0, ln_mlp_pre_0, ln_mlp_post_0, w_up_0, w_down_0,
           ln_mix_pre_1, ln_mix_post_1, w_in_1, w_out_1, ln_mlp_pre_1, ln_mlp_post_1, w_up_1, w_down_1):
    batch, seq, d = x.shape
    row = lambda g: g.reshape(1, d).astype(F32)

    n_qk0 = 2 * RET_HEADS * RET_DK
    w_qk_0 = _deinterleave_heads(w_in_0[:, :n_qk0].astype(BF16), 2 * RET_HEADS, RET_DK)
    w_rest_0 = w_in_0[:, n_qk0:].astype(BF16)
    n_qk1 = 2 * MOBA_HEADS * MOBA_DH
    w_qk_1 = _spread_rope_halves(w_in_1[:, :n_qk1].astype(BF16), 2 * MOBA_HEADS, MOBA_DH)
    w_rest_1 = w_in_1[:, n_qk1:].astype(BF16)
    log_gamma = jnp.log(1.0 - 2.0 ** (-5.0 - jnp.arange(RET_HEADS, dtype=F32)))

    h = x.reshape(batch * seq, d)

    qk, vg = _inproj(h, row(ln_mix_pre_0), w_qk_0, w_rest_0, _ret_tables(seq), mode="ret", batch=batch, seq=seq)
    mix = _retention(qk, vg, log_gamma, batch, seq)
    h = _outproj_mlp(mix.reshape(batch * seq, -1), h, w_out_0.astype(BF16), row(ln_mix_post_0),
                     row(ln_mlp_pre_0), row(ln_mlp_post_0), w_up_0.astype(BF16), w_down_0.astype(BF16))

    (qkv,) = _inproj(h, row(ln_mix_pre_1), w_qk_1, w_rest_1, _moba_tables(seq), mode="moba", batch=batch, seq=seq)
    mix = _moba(qkv, batch, seq)
    h = _outproj_mlp(mix.reshape(batch * seq, -1), h, w_out_1.astype(BF16), row(ln_mix_post_1),
                     row(ln_mlp_pre_1), row(ln_mlp_post_1), w_up_1.astype(BF16), w_down_1.astype(BF16))
    return h.reshape(batch, seq, d)
```

```python
import functools
import math

import jax
import jax.numpy as jnp
from jax import lax
from jax.experimental import pallas as pl
from jax.experimental.pallas import tpu as pltpu

D_MODEL = 1024
D_FF = 4 * D_MODEL
NORM_EPS = 1e-6
NEG = -1e30
LOG2E = math.log2(math.e)

RET_HEADS = 4
RET_DK = 256
RET_DV = 512
RET_THETA = 10000.0
RET_CHUNK = 256

MOBA_HEADS = 8
MOBA_DH = 128
MOBA_BLOCK = 256
MOBA_TOPK = 3
ROPE_THETA = 500000.0
ROPE_DIM = 32

LANES = 128
BF16_SUBLANES = 16
VMEM_LIMIT = 60 * 1024 * 1024

F32 = jnp.float32
BF16 = jnp.bfloat16
NT_DIMS = (((1,), (1,)), ((), ()))


def _rms_norm(x, g):
    return x * lax.rsqrt(jnp.mean(x * x, axis=-1, keepdims=True) + NORM_EPS) * g


def _resident(shape):
    return pl.BlockSpec(shape, lambda *_: (0,) * len(shape), pipeline_mode=pl.Buffered(1))


def _inproj_outputs(mode):
    if mode == "ret":
        return [(2 * RET_HEADS, RET_DK), (2 * RET_HEADS, RET_DV)]
    return [(3 * MOBA_HEADS, MOBA_DH)]


def _inproj_plan(mode):
    if mode == "ret":
        hq = RET_HEADS * RET_DK
        hv = RET_HEADS * RET_DV
        plan = [(1, hv + c * RET_DV, RET_DV, "swish", 1, RET_HEADS + c) for c in range(RET_HEADS)]
        plan += [(0, h * RET_DK, RET_DK, 1.0, 0, h) for h in range(RET_HEADS)]
        plan += [(0, hq + h * RET_DK, RET_DK, RET_DK ** -0.5, 0, RET_HEADS + h) for h in range(RET_HEADS)]
        plan += [(1, c * RET_DV, RET_DV, None, 1, c) for c in range(RET_HEADS)]
    else:
        hq = MOBA_HEADS * MOBA_DH
        per = 4
        cw = per * MOBA_DH
        plan = [(0, c * cw, cw, MOBA_DH ** -0.5 * LOG2E, 0, c * per) for c in range(hq // cw)]
        plan += [(0, hq + c * cw, cw, 1.0, 0, MOBA_HEADS + c * per) for c in range(hq // cw)]
        plan += [(1, c * cw, cw, None, 0, 2 * MOBA_HEADS + c * per) for c in range(hq // cw)]
    return plan


def _inproj_kernel(x_ref, g_ref, wqk_ref, wrest_ref, cos_ref, sin_ref, *rest, mode):
    out_refs, xs_ref = rest[:-1], rest[-1]
    xs_ref[...] = _rms_norm(x_ref[...], g_ref[...]).astype(BF16)
    cos, sin = cos_ref[...], sin_ref[...]
    scaled = {}
    for wi, start, width, scale, oi, slab0 in _inproj_plan(mode):
        o_ref = out_refs[oi]
        sw = o_ref.shape[-1]
        w_ref = (wqk_ref, wrest_ref)[wi]
        acc = jnp.dot(xs_ref[...], w_ref[:, start:start + width], preferred_element_type=F32)
        if scale is None or scale == "swish":
            if scale == "swish":
                half = 0.5 * acc
                acc = half + half * jnp.tanh(half)
            for i in range(width // sw):
                o_ref[0, slab0 + i] = acc[:, i * sw:(i + 1) * sw].astype(o_ref.dtype)
            continue
        if scale not in scaled:
            scaled[scale] = (cos * scale, sin * scale)
        c, s = scaled[scale]
        if mode == "ret":
            x1, x2 = acc[:, :LANES], acc[:, LANES:]
            o_ref[0, slab0, :, 0:LANES] = (x1 * c - x2 * s).astype(o_ref.dtype)
            o_ref[0, slab0, :, LANES:width] = (x1 * s + x2 * c).astype(o_ref.dtype)
        else:
            for i in range(width // sw):
                xh = acc[:, i * sw:(i + 1) * sw]
                y = xh * c + pltpu.roll(xh, LANES // 2, 1) * s
                o_ref[0, slab0 + i] = y.astype(o_ref.dtype)


def _inproj(x2d, gain, w_qk, w_rest, tabs, *, mode, batch, seq, tm=1024):
    m_total, d = x2d.shape
    tiles_per_seq = seq // tm
    tab_spec = pl.BlockSpec((tm, LANES), lambda m: (m % tiles_per_seq, 0))
    outs = _inproj_outputs(mode)
    return pl.pallas_call(
        functools.partial(_inproj_kernel, mode=mode),
        grid=(m_total // tm,),
        in_specs=[pl.BlockSpec((tm, d), lambda m: (m, 0)), _resident((1, d)),
                  _resident(w_qk.shape), _resident(w_rest.shape), tab_spec, tab_spec],
        out_specs=[pl.BlockSpec((1, ns, tm, sw), lambda m: (m // tiles_per_seq, 0, m % tiles_per_seq, 0))
                   for ns, sw in outs],
        out_shape=[jax.ShapeDtypeStruct((batch, ns, seq, sw), BF16) for ns, sw in outs],
        scratch_shapes=[pltpu.VMEM((tm, d), BF16)],
        compiler_params=pltpu.CompilerParams(
            dimension_semantics=("arbitrary",), vmem_limit_bytes=VMEM_LIMIT),
        name=f"inproj_{mode}",
    )(x2d, gain, w_qk, w_rest, *tabs)


def _retention_kernel(lg_ref, q_ref, k_ref, v_ref, g_ref, o_ref, state_ref):
    L = RET_CHUNK
    seq = q_ref.shape[0]
    lg = lg_ref[pl.program_id(1)]

    ii = lax.broadcasted_iota(jnp.int32, (L, L), 0)
    jj = lax.broadcasted_iota(jnp.int32, (L, L), 1)
    diff = (ii - jj).astype(F32)
    d_intra = jnp.where(diff >= 0, jnp.exp(lg * jnp.maximum(diff, 0.0)), 0.0)
    col = lax.broadcasted_iota(jnp.int32, (L, 1), 0).astype(F32)
    q_decay = jnp.exp(lg * (col + 1.0))
    row = lax.broadcasted_iota(jnp.int32, (1, L), 1).astype(F32)
    k_decay = jnp.exp(lg * (L - 1.0 - row))
    chunk_decay = jnp.exp(jnp.full((1, 1), lg * L, F32))

    state_ref[...] = jnp.zeros_like(state_ref)
    for c in range(seq // L):
        rows = pl.ds(c * L, L)
        qc = q_ref[rows, :]
        kc = k_ref[rows, :]
        vc = v_ref[rows, :]
        scores = lax.dot_general(qc, kc, NT_DIMS, preferred_element_type=F32) * d_intra
        o = jnp.dot(scores.astype(BF16), vc, preferred_element_type=F32)
        state = state_ref[...]
        o = o + jnp.dot(qc, state.astype(BF16), preferred_element_type=F32) * q_decay
        kd_t = (kc.astype(F32).T * k_decay).astype(BF16)
        state_ref[...] = state * chunk_decay + jnp.dot(kd_t, vc, preferred_element_type=F32)
        o = o * lax.rsqrt(jnp.mean(o * o, axis=-1, keepdims=True) + NORM_EPS)
        o_ref[rows, :] = (o * g_ref[rows, :].astype(F32)).astype(o_ref.dtype)


def _retention(qk, vg, log_gamma, batch, seq):
    H, DK, DV = RET_HEADS, RET_DK, RET_DV
    grid_spec = pltpu.PrefetchScalarGridSpec(
        num_scalar_prefetch=1,
        grid=(batch, H),
        in_specs=[
            pl.BlockSpec((None, None, seq, DK), lambda b, h, lg: (b, h, 0, 0)),
            pl.BlockSpec((None, None, seq, DK), lambda b, h, lg: (b, H + h, 0, 0)),
            pl.BlockSpec((None, None, seq, DV), lambda b, h, lg: (b, h, 0, 0)),
            pl.BlockSpec((None, None, seq, DV), lambda b, h, lg: (b, H + h, 0, 0)),
        ],
        out_specs=pl.BlockSpec((None, seq, DV), lambda b, h, lg: (b, 0, h)),
        scratch_shapes=[pltpu.VMEM((DK, DV), F32)],
    )
    return pl.pallas_call(
        _retention_kernel,
        grid_spec=grid_spec,
        out_shape=jax.ShapeDtypeStruct((batch, seq, H * DV), BF16),
        compiler_params=pltpu.CompilerParams(
            dimension_semantics=("arbitrary", "arbitrary"), vmem_limit_bytes=VMEM_LIMIT),
        name="retention",
    )(log_gamma, qk, qk, vg, vg)


def _moba_kernel(q_ref, k_ref, v_ref, o_ref, ka_ref, ksum_ref, vt_ref, s0_ref, s1_ref, p0_ref, p1_ref):
    s_bufs, p_bufs = (s0_ref, s1_ref), (p0_ref, p1_ref)
    BS = MOBA_BLOCK
    seq, dh = q_ref.shape
    nb = seq // BS
    G = ka_ref.shape[0] - seq

    for j in range(nb):
        ksum = jnp.sum(k_ref[pl.ds(j * BS, BS), :].astype(F32), axis=0, keepdims=True)
        hi = ksum.astype(BF16).astype(F32)
        ksum_ref[j:j + 1, :] = hi
        ksum_ref[nb + j:nb + j + 1, :] = ksum - hi
    ka_ref[0:G, :] = ksum_ref[...].astype(BF16)
    ka_ref[G:, :] = k_ref[...]
    gate_floor = NEG * BS * LOG2E

    QT = 2
    order = list(range(nb // QT - 1, -1, -1))

    def scores(i):
        t = order[i]
        nr = G + (t + 1) * QT * BS
        s_bufs[i % 2][0:nr, :] = lax.dot_general(ka_ref[0:nr, :], q_ref[pl.ds(t * QT * BS, QT * BS), :],
                                                 NT_DIMS, preferred_element_type=F32)

    scores(0)

    for j in range(nb):
        cols = pl.ds(j * BS, BS)
        vt_ref[0:dh, cols] = v_ref[cols, :].astype(F32).T.astype(BF16)
    vt_ref[dh:, :] = jnp.ones((vt_ref.shape[0] - dh, seq), BF16)

    key_idx = lax.broadcasted_iota(jnp.int32, (BS, BS), 0)
    qry_idx = lax.broadcasted_iota(jnp.int32, (BS, BS), 1)
    causal = key_idx <= qry_idx

    for i, t in enumerate(order):
        nk = (t + 1) * QT * BS
        if i + 1 < len(order):
            scores(i + 1)
        s_cur, p_cur = s_bufs[i % 2], p_bufs[i % 2]

        for hf in range(QT):
            n = t * QT + hf
            cs = pl.ds(hf * BS, BS)

            gate = [s_cur[j:j + 1, cs] + s_cur[nb + j:nb + j + 1, cs] for j in range(n)]

            sels = []
            for j in range(n):
                rank = jnp.where(gate[j] < gate_floor, float(nb - n), 0.0)
                for r in range(n):
                    if r != j:
                        ahead = (gate[r] >= gate[j]) if r < j else (gate[r] > gate[j])
                        rank = rank + ahead.astype(F32)
                sels.append(rank < MOBA_TOPK)

            s_own = jnp.where(causal, s_cur[pl.ds(G + n * BS, BS), cs], NEG)
            m = jnp.max(s_own, axis=0, keepdims=True)
            for j in range(n):
                m_blk = jnp.max(s_cur[pl.ds(G + j * BS, BS), cs], axis=0, keepdims=True)
                m = jnp.maximum(m, jnp.where(sels[j], m_blk, NEG))

            p_cur[pl.ds(n * BS, BS), cs] = jnp.exp2(s_own - m).astype(BF16)
            for j in range(n):
                p = jnp.exp2(s_cur[pl.ds(G + j * BS, BS), cs] - jnp.where(sels[j], m, -NEG))
                p_cur[pl.ds(j * BS, BS), cs] = p.astype(BF16)
            for j in range(n + 1, (t + 1) * QT):
                p_cur[pl.ds(j * BS, BS), cs] = jnp.zeros((BS, BS), BF16)

        acc = jnp.dot(vt_ref[:, 0:nk], p_cur[0:nk, :], preferred_element_type=F32)
        for hf in range(QT):
            cs = slice(hf * BS, (hf + 1) * BS)
            out = acc[0:dh, cs] * (1.0 / acc[dh:dh + 1, cs])
            o_ref[pl.ds((t * QT + hf) * BS, BS), :] = out.T.astype(o_ref.dtype)


def _moba(qkv, batch, seq):
    H, DH = MOBA_HEADS, MOBA_DH
    assert 2 * (seq // MOBA_BLOCK) == BF16_SUBLANES, "hi and lo key-sum rows must fill one bf16 tile"
    return pl.pallas_call(
        _moba_kernel,
        grid=(batch, H),
        in_specs=[
            pl.BlockSpec((None, None, seq, DH), lambda b, h: (b, h, 0, 0)),
            pl.BlockSpec((None, None, seq, DH), lambda b, h: (b, H + h, 0, 0)),
            pl.BlockSpec((None, None, seq, DH), lambda b, h: (b, 2 * H + h, 0, 0)),
        ],
        out_specs=pl.BlockSpec((None, seq, DH), lambda b, h: (b, 0, h)),
        out_shape=jax.ShapeDtypeStruct((batch, seq, H * DH), BF16),
        scratch_shapes=[pltpu.VMEM((BF16_SUBLANES + seq, DH), BF16),
                        pltpu.VMEM((BF16_SUBLANES, DH), F32),
                        pltpu.VMEM((DH + BF16_SUBLANES, seq), BF16),
                        pltpu.VMEM((BF16_SUBLANES + seq, 2 * MOBA_BLOCK), F32),
                        pltpu.VMEM((BF16_SUBLANES + seq, 2 * MOBA_BLOCK), F32),
                        pltpu.VMEM((seq, 2 * MOBA_BLOCK), BF16), pltpu.VMEM((seq, 2 * MOBA_BLOCK), BF16)],
        compiler_params=pltpu.CompilerParams(
            dimension_semantics=("arbitrary", "arbitrary"), vmem_limit_bytes=VMEM_LIMIT),
        name="moba",
    )(qkv, qkv, qkv)


def _outproj_mlp_kernel(a_ref, h_ref, wo_ref, gpost_ref, gpre_ref, gmpost_ref, wup_ref, wdn_ref,
                        o_ref, xn_ref, hid_ref, *, sub, ff_chunk):
    assert sum(sub) == a_ref.shape[0]
    subs = [pl.ds(sum(sub[:i]), sub[i]) for i in range(len(sub))]
    for rows in subs:
        mix = jnp.dot(a_ref[rows, :], wo_ref[...], preferred_element_type=F32)
        h1 = h_ref[rows, :] + _rms_norm(mix, gpost_ref[...])
        o_ref[rows, :] = h1
        xn_ref[rows, :] = _rms_norm(h1, gpre_ref[...]).astype(BF16)
    for rows in subs:
        for j in range(wup_ref.shape[1] // ff_chunk):
            cols = pl.ds(j * ff_chunk, ff_chunk)
            u = jnp.maximum(jnp.dot(xn_ref[rows, :], wup_ref[:, cols], preferred_element_type=F32), 0.0)
            hid_ref[rows, cols] = (u * u).astype(BF16)
    for rows in subs:
        dn = jnp.dot(hid_ref[rows, :], wdn_ref[...], preferred_element_type=F32)
        o_ref[rows, :] = o_ref[rows, :] + _rms_norm(dn, gmpost_ref[...])


def _outproj_mlp(a2d, h2d, w_out, g_post, g_mpre, g_mpost, w_up, w_down, *, tm=1024, sub=(256, 256, 256, 256), ff_chunk=1024):
    m_total, d = h2d.shape
    ka = a2d.shape[1]
    dff = w_up.shape[1]
    return pl.pallas_call(
        functools.partial(_outproj_mlp_kernel, sub=sub, ff_chunk=ff_chunk),
        grid=(m_total // tm,),
        in_specs=[
            pl.BlockSpec((tm, ka), lambda m: (m, 0)),
            pl.BlockSpec((tm, d), lambda m: (m, 0)),
            _resident((ka, d)),
            _resident((1, d)),
            _resident((1, d)),
            _resident((1, d)),
            _resident((d, dff)),
            _resident((dff, d)),
        ],
        out_specs=pl.BlockSpec((tm, d), lambda m: (m, 0)),
        out_shape=jax.ShapeDtypeStruct((m_total, d), F32),
        scratch_shapes=[pltpu.VMEM((tm, d), BF16), pltpu.VMEM((tm, dff), BF16)],
        compiler_params=pltpu.CompilerParams(
            dimension_semantics=("arbitrary",), vmem_limit_bytes=VMEM_LIMIT),
        name=f"outproj_mlp_k{ka}",
    )(a2d, h2d, w_out, g_post, g_mpre, g_mpost, w_up, w_down)


def _ret_tables(seq):
    freq = 1.0 / (RET_THETA ** jnp.linspace(0.0, 1.0, RET_DK // 2, dtype=F32))
    ang = jnp.arange(seq)[:, None].astype(F32) * freq[None, :]
    return jnp.cos(ang), jnp.sin(ang)


def _moba_tables(seq):
    half = ROPE_DIM // 2
    inv = ROPE_THETA ** (-jnp.arange(0, ROPE_DIM, 2, dtype=F32) / ROPE_DIM)
    ang = jnp.arange(seq)[:, None].astype(F32) * inv[None, :]
    cos, sin = jnp.cos(ang), jnp.sin(ang)
    gap = LANES // 2 - half
    c = jnp.concatenate([cos, jnp.ones((seq, gap), F32), cos, jnp.ones((seq, gap), F32)], axis=1)
    s = jnp.concatenate([-sin, jnp.zeros((seq, gap), F32), sin, jnp.zeros((seq, gap), F32)], axis=1)
    return c, s


def _deinterleave_heads(w, n_heads, dk):
    d = w.shape[0]
    return w.reshape(d, n_heads, dk // 2, 2).transpose(0, 1, 3, 2).reshape(d, n_heads * dk)


def _spread_rope_halves(w, n_heads, dh):
    d = w.shape[0]
    half = ROPE_DIM // 2
    gap = dh // 2 - half
    w = w.reshape(d, n_heads, dh)
    x1, x2, rest = w[..., :half], w[..., half:ROPE_DIM], w[..., ROPE_DIM:]
    return jnp.concatenate([x1, rest[..., :gap], x2, rest[..., gap:]], axis=-1).reshape(d, n_heads * dh)


def kernel(x, ln_mix_pre_0, ln_mix_post_0, w_in_0, w_out_0, ln_mlp_pre_0, ln_mlp_post_0, w_up_0, w_down_0,
           ln_mix_pre_1, ln_mix_post_1, w_in_1, w_out_1, ln_mlp_pre_1, ln_mlp_post_1, w_up_1, w_down_1):
    batch, seq, d = x.shape
    row = lambda g: g.reshape(1, d).astype(F32)

    n_qk0 = 2 * RET_HEADS * RET_DK
    w_qk_0 = _deinterleave_heads(w_in_0[:, :n_qk0].astype(BF16), 2 * RET_HEADS, RET_DK)
    w_rest_0 = w_in_0[:, n_qk0:].astype(BF16)
    n_qk1 = 2 * MOBA_HEADS * MOBA_DH
    w_qk_1 = _spread_rope_halves(w_in_1[:, :n_qk1].astype(BF16), 2 * MOBA_HEADS, MOBA_DH)
    w_rest_1 = w_in_1[:, n_qk1:].astype(BF16)
    log_gamma = jnp.log(1.0 - 2.0 ** (-5.0 - jnp.arange(RET_HEADS, dtype=F32)))

    h = x.reshape(batch * seq, d)

    qk, vg = _inproj(h, row(ln_mix_pre_0), w_qk_0, w_rest_0, _ret_tables(seq), mode="ret", batch=batch, seq=seq)
    mix = _retention(qk, vg, log_gamma, batch, seq)
    h = _outproj_mlp(mix.reshape(batch * seq, -1), h, w_out_0.astype(BF16), row(ln_mix_post_0),
                     row(ln_mlp_pre_0), row(ln_mlp_post_0), w_up_0.astype(BF16), w_down_0.astype(BF16))

    (qkv,) = _inproj(h, row(ln_mix_pre_1), w_qk_1, w_rest_1, _moba_tables(seq), mode="moba", batch=batch, seq=seq)
    mix = _moba(qkv, batch, seq)
    h = _outproj_mlp(mix.reshape(batch * seq, -1), h, w_out_1.astype(BF16), row(ln_mix_post_1),
                     row(ln_mlp_pre_1), row(ln_mlp_post_1), w_up_1.astype(BF16), w_down_1.astype(BF16))
    return h.reshape(batch, seq, d)
```

```python
import functools
import math

import jax
import jax.numpy as jnp
from jax import lax
from jax.experimental import pallas as pl
from jax.experimental.pallas import tpu as pltpu

D_MODEL = 1024
D_FF = 4 * D_MODEL
NORM_EPS = 1e-6
NEG = -1e30
LOG2E = math.log2(math.e)

RET_HEADS = 4
RET_DK = 256
RET_DV = 512
RET_THETA = 10000.0
RET_CHUNK = 256

MOBA_HEADS = 8
MOBA_DH = 128
MOBA_BLOCK = 256
MOBA_TOPK = 3
ROPE_THETA = 500000.0
ROPE_DIM = 32

LANES = 128
BF16_SUBLANES = 16
VMEM_LIMIT = 60 * 1024 * 1024

F32 = jnp.float32
BF16 = jnp.bfloat16
NT_DIMS = (((1,), (1,)), ((), ()))


def _rms_norm(x, g):
    return x * lax.rsqrt(jnp.mean(x * x, axis=-1, keepdims=True) + NORM_EPS) * g


def _resident(shape):
    return pl.BlockSpec(shape, lambda *_: (0,) * len(shape), pipeline_mode=pl.Buffered(1))


def _inproj_outputs(mode):
    if mode == "ret":
        return [(2 * RET_HEADS, RET_DK), (2 * RET_HEADS, RET_DV)]
    return [(3 * MOBA_HEADS, MOBA_DH)]


def _inproj_plan(mode):
    if mode == "ret":
        hq = RET_HEADS * RET_DK
        hv = RET_HEADS * RET_DV
        plan = [(1, hv + c * RET_DV, RET_DV, "swish", 1, RET_HEADS + c) for c in range(RET_HEADS)]
        plan += [(0, h * RET_DK, RET_DK, 1.0, 0, h) for h in range(RET_HEADS)]
        plan += [(0, hq + h * RET_DK, RET_DK, RET_DK ** -0.5, 0, RET_HEADS + h) for h in range(RET_HEADS)]
        plan += [(1, c * RET_DV, RET_DV, None, 1, c) for c in range(RET_HEADS)]
    else:
        hq = MOBA_HEADS * MOBA_DH
        per = 4
        cw = per * MOBA_DH
        plan = [(0, c * cw, cw, MOBA_DH ** -0.5 * LOG2E, 0, c * per) for c in range(hq // cw)]
        plan += [(0, hq + c * cw, cw, 1.0, 0, MOBA_HEADS + c * per) for c in range(hq // cw)]
        plan += [(1, c * cw, cw, None, 0, 2 * MOBA_HEADS + c * per) for c in range(hq // cw)]
    return plan


PERM_W = 256


def _qk_column_permutation(mode):
    new = lax.broadcasted_iota(jnp.int32, (PERM_W, PERM_W), 1)
    old = lax.broadcasted_iota(jnp.int32, (PERM_W, PERM_W), 0)
    if mode == "ret":
        half = RET_DK // 2
        src = jnp.where(new < half, 2 * new, 2 * (new - half) + 1)
    else:
        j = new % MOBA_DH
        h16, mid = ROPE_DIM // 2, MOBA_DH // 2
        src_j = jnp.where(j < h16, j, jnp.where(j < mid, j + h16, jnp.where(j < mid + h16, j - (mid - h16), j)))
        src = (new - j) + src_j
    return (old == src).astype(BF16)


def _inproj_kernel(x_ref, g_ref, w_ref, cos_ref, sin_ref, *rest, mode):
    out_refs, xs_ref, wqk_ref = rest[:-2], rest[-2], rest[-1]
    n_qk = wqk_ref.shape[1]

    @pl.when(pl.program_id(0) == 0)
    def _():
        perm = _qk_column_permutation(mode)
        for c0 in range(0, n_qk, PERM_W):
            cols = slice(c0, c0 + PERM_W)
            wqk_ref[:, cols] = jnp.dot(w_ref[:, cols], perm, preferred_element_type=F32).astype(BF16)

    xs_ref[...] = _rms_norm(x_ref[...], g_ref[...]).astype(BF16)
    cos, sin = cos_ref[...], sin_ref[...]
    scaled = {}
    for wi, start, width, scale, oi, slab0 in _inproj_plan(mode):
        o_ref = out_refs[oi]
        sw = o_ref.shape[-1]
        w_cols = wqk_ref[:, start:start + width] if wi == 0 else w_ref[:, n_qk + start:n_qk + start + width]
        acc = jnp.dot(xs_ref[...], w_cols, preferred_element_type=F32)
        if scale is None or scale == "swish":
            if scale == "swish":
                half = 0.5 * acc
                acc = half + half * jnp.tanh(half)
            for i in range(width // sw):
                o_ref[0, slab0 + i] = acc[:, i * sw:(i + 1) * sw].astype(o_ref.dtype)
            continue
        if scale not in scaled:
            scaled[scale] = (cos * scale, sin * scale)
        c, s = scaled[scale]
        if mode == "ret":
            x1, x2 = acc[:, :LANES], acc[:, LANES:]
            o_ref[0, slab0, :, 0:LANES] = (x1 * c - x2 * s).astype(o_ref.dtype)
            o_ref[0, slab0, :, LANES:width] = (x1 * s + x2 * c).astype(o_ref.dtype)
        else:
            for i in range(width // sw):
                xh = acc[:, i * sw:(i + 1) * sw]
                y = xh * c + pltpu.roll(xh, LANES // 2, 1) * s
                o_ref[0, slab0 + i] = y.astype(o_ref.dtype)


def _inproj(x2d, gain, w, tabs, *, mode, batch, seq, tm=1024):
    m_total, d = x2d.shape
    n_qk = 2 * (RET_HEADS * RET_DK if mode == "ret" else MOBA_HEADS * MOBA_DH)
    tiles_per_seq = seq // tm
    tab_spec = pl.BlockSpec((tm, LANES), lambda m: (m % tiles_per_seq, 0))
    outs = _inproj_outputs(mode)
    return pl.pallas_call(
        functools.partial(_inproj_kernel, mode=mode),
        grid=(m_total // tm,),
        in_specs=[pl.BlockSpec((tm, d), lambda m: (m, 0)), _resident((1, d)), _resident(w.shape),
                  tab_spec, tab_spec],
        out_specs=[pl.BlockSpec((1, ns, tm, sw), lambda m: (m // tiles_per_seq, 0, m % tiles_per_seq, 0))
                   for ns, sw in outs],
        out_shape=[jax.ShapeDtypeStruct((batch, ns, seq, sw), BF16) for ns, sw in outs],
        scratch_shapes=[pltpu.VMEM((tm, d), BF16), pltpu.VMEM((d, n_qk), BF16)],
        compiler_params=pltpu.CompilerParams(
            dimension_semantics=("arbitrary",), vmem_limit_bytes=VMEM_LIMIT),
        name=f"inproj_{mode}",
    )(x2d, gain, w, *tabs)


def _retention_kernel(lg_ref, q_ref, k_ref, v_ref, g_ref, o_ref, state_ref):
    L = RET_CHUNK
    seq = q_ref.shape[0]
    lg = lg_ref[pl.program_id(1)]

    ii = lax.broadcasted_iota(jnp.int32, (L, L), 0)
    jj = lax.broadcasted_iota(jnp.int32, (L, L), 1)
    diff = (ii - jj).astype(F32)
    d_intra = jnp.where(diff >= 0, jnp.exp(lg * jnp.maximum(diff, 0.0)), 0.0)
    col = lax.broadcasted_iota(jnp.int32, (L, 1), 0).astype(F32)
    q_decay = jnp.exp(lg * (col + 1.0))
    row = lax.broadcasted_iota(jnp.int32, (1, L), 1).astype(F32)
    k_decay = jnp.exp(lg * (L - 1.0 - row))
    chunk_decay = jnp.exp(jnp.full((1, 1), lg * L, F32))

    state_ref[...] = jnp.zeros_like(state_ref)
    for c in range(seq // L):
        rows = pl.ds(c * L, L)
        qc = q_ref[rows, :]
        kc = k_ref[rows, :]
        vc = v_ref[rows, :]
        scores = lax.dot_general(qc, kc, NT_DIMS, preferred_element_type=F32) * d_intra
        o = jnp.dot(scores.astype(BF16), vc, preferred_element_type=F32)
        state = state_ref[...]
        o = o + jnp.dot(qc, state.astype(BF16), preferred_element_type=F32) * q_decay
        kd_t = (kc.astype(F32).T * k_decay).astype(BF16)
        state_ref[...] = state * chunk_decay + jnp.dot(kd_t, vc, preferred_element_type=F32)
        o = o * lax.rsqrt(jnp.mean(o * o, axis=-1, keepdims=True) + NORM_EPS)
        o_ref[rows, :] = (o * g_ref[rows, :].astype(F32)).astype(o_ref.dtype)


def _retention(qk, vg, log_gamma, batch, seq):
    H, DK, DV = RET_HEADS, RET_DK, RET_DV
    grid_spec = pltpu.PrefetchScalarGridSpec(
        num_scalar_prefetch=1,
        grid=(batch, H),
        in_specs=[
            pl.BlockSpec((None, None, seq, DK), lambda b, h, lg: (b, h, 0, 0)),
            pl.BlockSpec((None, None, seq, DK), lambda b, h, lg: (b, H + h, 0, 0)),
            pl.BlockSpec((None, None, seq, DV), lambda b, h, lg: (b, h, 0, 0)),
            pl.BlockSpec((None, None, seq, DV), lambda b, h, lg: (b, H + h, 0, 0)),
        ],
        out_specs=pl.BlockSpec((None, seq, DV), lambda b, h, lg: (b, 0, h)),
        scratch_shapes=[pltpu.VMEM((DK, DV), F32)],
    )
    return pl.pallas_call(
        _retention_kernel,
        grid_spec=grid_spec,
        out_shape=jax.ShapeDtypeStruct((batch, seq, H * DV), BF16),
        compiler_params=pltpu.CompilerParams(
            dimension_semantics=("arbitrary", "arbitrary"), vmem_limit_bytes=VMEM_LIMIT),
        name="retention",
    )(log_gamma, qk, qk, vg, vg)


def _moba_kernel(q_ref, k_ref, v_ref, o_ref, ka_ref, ksum_ref, vt_ref, s0_ref, s1_ref, p0_ref, p1_ref):
    s_bufs, p_bufs = (s0_ref, s1_ref), (p0_ref, p1_ref)
    BS = MOBA_BLOCK
    seq, dh = q_ref.shape
    nb = seq // BS
    G = ka_ref.shape[0] - seq

    for j in range(nb):
        ksum = jnp.sum(k_ref[pl.ds(j * BS, BS), :].astype(F32), axis=0, keepdims=True)
        hi = ksum.astype(BF16).astype(F32)
        ksum_ref[j:j + 1, :] = hi
        ksum_ref[nb + j:nb + j + 1, :] = ksum - hi
    ka_ref[0:G, :] = ksum_ref[...].astype(BF16)
    ka_ref[G:, :] = k_ref[...]
    gate_floor = NEG * BS * LOG2E

    QT = 2
    order = list(range(nb // QT - 1, -1, -1))

    def scores(i):
        t = order[i]
        nr = G + (t + 1) * QT * BS
        s_bufs[i % 2][0:nr, :] = lax.dot_general(ka_ref[0:nr, :], q_ref[pl.ds(t * QT * BS, QT * BS), :],
                                                 NT_DIMS, preferred_element_type=F32)

    scores(0)

    for j in range(nb):
        cols = pl.ds(j * BS, BS)
        vt_ref[0:dh, cols] = v_ref[cols, :].astype(F32).T.astype(BF16)
    vt_ref[dh:, :] = jnp.ones((vt_ref.shape[0] - dh, seq), BF16)

    key_idx = lax.broadcasted_iota(jnp.int32, (BS, BS), 0)
    qry_idx = lax.broadcasted_iota(jnp.int32, (BS, BS), 1)
    causal = key_idx <= qry_idx

    for i, t in enumerate(order):
        nk = (t + 1) * QT * BS
        if i + 1 < len(order):
            scores(i + 1)
        s_cur, p_cur = s_bufs[i % 2], p_bufs[i % 2]

        for hf in range(QT):
            n = t * QT + hf
            cs = pl.ds(hf * BS, BS)

            gate = [s_cur[j:j + 1, cs] + s_cur[nb + j:nb + j + 1, cs] for j in range(n)]

            sels = []
            for j in range(n):
                rank = jnp.where(gate[j] < gate_floor, float(nb - n), 0.0)
                for r in range(n):
                    if r != j:
                        ahead = (gate[r] >= gate[j]) if r < j else (gate[r] > gate[j])
                        rank = rank + ahead.astype(F32)
                sels.append(rank < MOBA_TOPK)

            s_own = jnp.where(causal, s_cur[pl.ds(G + n * BS, BS), cs], NEG)
            m = jnp.max(s_own, axis=0, keepdims=True)
            for j in range(n):
                m_blk = jnp.max(s_cur[pl.ds(G + j * BS, BS), cs], axis=0, keepdims=True)
                m = jnp.maximum(m, jnp.where(sels[j], m_blk, NEG))

            p_cur[pl.ds(n * BS, BS), cs] = jnp.exp2(s_own - m).astype(BF16)
            for j in range(n):
                p = jnp.exp2(s_cur[pl.ds(G + j * BS, BS), cs] - jnp.where(sels[j], m, -NEG))
                p_cur[pl.ds(j * BS, BS), cs] = p.astype(BF16)
            for j in range(n + 1, (t + 1) * QT):
                p_cur[pl.ds(j * BS, BS), cs] = jnp.zeros((BS, BS), BF16)

        acc = jnp.dot(vt_ref[:, 0:nk], p_cur[0:nk, :], preferred_element_type=F32)
        for hf in range(QT):
            cs = slice(hf * BS, (hf + 1) * BS)
            out = acc[0:dh, cs] * (1.0 / acc[dh:dh + 1, cs])
            o_ref[pl.ds((t * QT + hf) * BS, BS), :] = out.T.astype(o_ref.dtype)


def _moba(qkv, batch, seq):
    H, DH = MOBA_HEADS, MOBA_DH
    assert 2 * (seq // MOBA_BLOCK) == BF16_SUBLANES, "hi and lo key-sum rows must fill one bf16 tile"
    return pl.pallas_call(
        _moba_kernel,
        grid=(batch, H),
        in_specs=[
            pl.BlockSpec((None, None, seq, DH), lambda b, h: (b, h, 0, 0)),
            pl.BlockSpec((None, None, seq, DH), lambda b, h: (b, H + h, 0, 0)),
            pl.BlockSpec((None, None, seq, DH), lambda b, h: (b, 2 * H + h, 0, 0)),
        ],
        out_specs=pl.BlockSpec((None, seq, DH), lambda b, h: (b, 0, h)),
        out_shape=jax.ShapeDtypeStruct((batch, seq, H * DH), BF16),
        scratch_shapes=[pltpu.VMEM((BF16_SUBLANES + seq, DH), BF16),
                        pltpu.VMEM((BF16_SUBLANES, DH), F32),
                        pltpu.VMEM((DH + BF16_SUBLANES, seq), BF16),
                        pltpu.VMEM((BF16_SUBLANES + seq, 2 * MOBA_BLOCK), F32),
                        pltpu.VMEM((BF16_SUBLANES + seq, 2 * MOBA_BLOCK), F32),
                        pltpu.VMEM((seq, 2 * MOBA_BLOCK), BF16), pltpu.VMEM((seq, 2 * MOBA_BLOCK), BF16)],
        compiler_params=pltpu.CompilerParams(
            dimension_semantics=("arbitrary", "arbitrary"), vmem_limit_bytes=VMEM_LIMIT),
        name="moba",
    )(qkv, qkv, qkv)


def _outproj_mlp_kernel(a_ref, h_ref, wo_ref, gpost_ref, gpre_ref, gmpost_ref, wup_ref, wdn_ref,
                        o_ref, xn_ref, hid_ref, *, sub, ff_chunk):
    assert sum(sub) == a_ref.shape[0]
    subs = [pl.ds(sum(sub[:i]), sub[i]) for i in range(len(sub))]
    for rows in subs:
        mix = jnp.dot(a_ref[rows, :], wo_ref[...], preferred_element_type=F32)
        h1 = h_ref[rows, :] + _rms_norm(mix, gpost_ref[...])
        o_ref[rows, :] = h1
        xn_ref[rows, :] = _rms_norm(h1, gpre_ref[...]).astype(BF16)
    for rows in subs:
        for j in range(wup_ref.shape[1] // ff_chunk):
            cols = pl.ds(j * ff_chunk, ff_chunk)
            u = jnp.maximum(jnp.dot(xn_ref[rows, :], wup_ref[:, cols], preferred_element_type=F32), 0.0)
            hid_ref[rows, cols] = (u * u).astype(BF16)
    for rows in subs:
        dn = jnp.dot(hid_ref[rows, :], wdn_ref[...], preferred_element_type=F32)
        o_ref[rows, :] = o_ref[rows, :] + _rms_norm(dn, gmpost_ref[...])


def _outproj_mlp(a2d, h2d, w_out, g_post, g_mpre, g_mpost, w_up, w_down, *, tm=1024, sub=(256, 256, 256, 256), ff_chunk=1024):
    m_total, d = h2d.shape
    ka = a2d.shape[1]
    dff = w_up.shape[1]
    return pl.pallas_call(
        functools.partial(_outproj_mlp_kernel, sub=sub, ff_chunk=ff_chunk),
        grid=(m_total // tm,),
        in_specs=[
            pl.BlockSpec((tm, ka), lambda m: (m, 0)),
            pl.BlockSpec((tm, d), lambda m: (m, 0)),
            _resident((ka, d)),
            _resident((1, d)),
            _resident((1, d)),
            _resident((1, d)),
            _resident((d, dff)),
            _resident((dff, d)),
        ],
        out_specs=pl.BlockSpec((tm, d), lambda m: (m, 0)),
        out_shape=jax.ShapeDtypeStruct((m_total, d), F32),
        scratch_shapes=[pltpu.VMEM((tm, d), BF16), pltpu.VMEM((tm, dff), BF16)],
        compiler_params=pltpu.CompilerParams(
            dimension_semantics=("arbitrary",), vmem_limit_bytes=VMEM_LIMIT),
        name=f"outproj_mlp_k{ka}",
    )(a2d, h2d, w_out, g_post, g_mpre, g_mpost, w_up, w_down)


def _ret_tables(seq):
    freq = 1.0 / (RET_THETA ** jnp.linspace(0.0, 1.0, RET_DK // 2, dtype=F32))
    ang = jnp.arange(seq)[:, None].astype(F32) * freq[None, :]
    return jnp.cos(ang), jnp.sin(ang)


def _moba_tables(seq):
    half = ROPE_DIM // 2
    inv = ROPE_THETA ** (-jnp.arange(0, ROPE_DIM, 2, dtype=F32) / ROPE_DIM)
    ang = jnp.arange(seq)[:, None].astype(F32) * inv[None, :]
    cos, sin = jnp.cos(ang), jnp.sin(ang)
    gap = LANES // 2 - half
    c = jnp.concatenate([cos, jnp.ones((seq, gap), F32), cos, jnp.ones((seq, gap), F32)], axis=1)
    s = jnp.concatenate([-sin, jnp.zeros((seq, gap), F32), sin, jnp.zeros((seq, gap), F32)], axis=1)
    return c, s


def kernel(x, ln_mix_pre_0, ln_mix_post_0, w_in_0, w_out_0, ln_mlp_pre_0, ln_mlp_post_0, w_up_0, w_down_0,
           ln_mix_pre_1, ln_mix_post_1, w_in_1, w_out_1, ln_mlp_pre_1, ln_mlp_post_1, w_up_1, w_down_1):
    batch, seq, d = x.shape
    row = lambda g: g.reshape(1, d).astype(F32)

    log_gamma = jnp.log(1.0 - 2.0 ** (-5.0 - jnp.arange(RET_HEADS, dtype=F32)))

    h = x.reshape(batch * seq, d)

    qk, vg = _inproj(h, row(ln_mix_pre_0), w_in_0.astype(BF16), _ret_tables(seq), mode="ret", batch=batch, seq=seq)
    mix = _retention(qk, vg, log_gamma, batch, seq)
    h = _outproj_mlp(mix.reshape(batch * seq, -1), h, w_out_0.astype(BF16), row(ln_mix_post_0),
                     row(ln_mlp_pre_0), row(ln_mlp_post_0), w_up_0.astype(BF16), w_down_0.astype(BF16))

    (qkv,) = _inproj(h, row(ln_mix_pre_1), w_in_1.astype(BF16), _moba_tables(seq), mode="moba", batch=batch,
                     seq=seq)
    mix = _moba(qkv, batch, seq)
    h = _outproj_mlp(mix.reshape(batch * seq, -1), h, w_out_1.astype(BF16), row(ln_mix_post_1),
                     row(ln_mlp_pre_1), row(ln_mlp_post_1), w_up_1.astype(BF16), w_down_1.astype(BF16))
    return h.reshape(batch, seq, d)
```

```python
import functools
import math

import jax
import jax.numpy as jnp
from jax import lax
from jax.experimental import pallas as pl
from jax.experimental.pallas import tpu as pltpu

D_MODEL = 1024
D_FF = 4 * D_MODEL
NORM_EPS = 1e-6
NEG = -1e30
LOG2E = math.log2(math.e)

RET_HEADS = 4
RET_DK = 256
RET_DV = 512
RET_THETA = 10000.0
RET_CHUNK = 256

MOBA_HEADS = 8
MOBA_DH = 128
MOBA_BLOCK = 256
MOBA_TOPK = 3
ROPE_THETA = 500000.0
ROPE_DIM = 32

LANES = 128
BF16_SUBLANES = 16
VMEM_LIMIT = 60 * 1024 * 1024

F32 = jnp.float32
BF16 = jnp.bfloat16
NT_DIMS = (((1,), (1,)), ((), ()))


def _rms_norm(x, g):
    return x * lax.rsqrt(jnp.mean(x * x, axis=-1, keepdims=True) + NORM_EPS) * g


def _resident(shape):
    return pl.BlockSpec(shape, lambda *_: (0,) * len(shape), pipeline_mode=pl.Buffered(1))


def _inproj_outputs(mode):
    if mode == "ret":
        return [(2 * RET_HEADS, RET_DK), (2 * RET_HEADS, RET_DV)]
    return [(3 * MOBA_HEADS, MOBA_DH)]


def _inproj_plan(mode):
    if mode == "ret":
        hq = RET_HEADS * RET_DK
        hv = RET_HEADS * RET_DV
        plan = [(1, hv + c * RET_DV, RET_DV, "swish", 1, RET_HEADS + c) for c in range(RET_HEADS)]
        plan += [(0, h * RET_DK, RET_DK, 1.0, 0, h) for h in range(RET_HEADS)]
        plan += [(0, hq + h * RET_DK, RET_DK, RET_DK ** -0.5, 0, RET_HEADS + h) for h in range(RET_HEADS)]
        plan += [(1, c * RET_DV, RET_DV, None, 1, c) for c in range(RET_HEADS)]
    else:
        hq = MOBA_HEADS * MOBA_DH
        per = 4
        cw = per * MOBA_DH
        plan = [(0, c * cw, cw, MOBA_DH ** -0.5 * LOG2E, 0, c * per) for c in range(hq // cw)]
        plan += [(0, hq + c * cw, cw, 1.0, 0, MOBA_HEADS + c * per) for c in range(hq // cw)]
        plan += [(1, c * cw, cw, None, 0, 2 * MOBA_HEADS + c * per) for c in range(hq // cw)]
    return plan


PERM_W = 256


def _qk_column_permutation(mode):
    new = lax.broadcasted_iota(jnp.int32, (PERM_W, PERM_W), 1)
    old = lax.broadcasted_iota(jnp.int32, (PERM_W, PERM_W), 0)
    if mode == "ret":
        half = RET_DK // 2
        src = jnp.where(new < half, 2 * new, 2 * (new - half) + 1)
    else:
        j = new % MOBA_DH
        h16, mid = ROPE_DIM // 2, MOBA_DH // 2
        src_j = jnp.where(j < h16, j, jnp.where(j < mid, j + h16, jnp.where(j < mid + h16, j - (mid - h16), j)))
        src = (new - j) + src_j
    return (old == src).astype(BF16)


def _inproj_kernel(x_ref, g_ref, w_ref, cos_ref, sin_ref, *rest, mode):
    out_refs, xs_ref, wqk_ref = rest[:-2], rest[-2], rest[-1]
    n_qk = wqk_ref.shape[1]

    @pl.when(pl.program_id(0) == 0)
    def _():
        perm = _qk_column_permutation(mode)
        for c0 in range(0, n_qk, PERM_W):
            cols = slice(c0, c0 + PERM_W)
            wqk_ref[:, cols] = jnp.dot(w_ref[:, cols], perm, preferred_element_type=F32).astype(BF16)

    xs_ref[...] = _rms_norm(x_ref[...], g_ref[...]).astype(BF16)
    cos, sin = cos_ref[...], sin_ref[...]
    scaled = {}
    for wi, start, width, scale, oi, slab0 in _inproj_plan(mode):
        o_ref = out_refs[oi]
        sw = o_ref.shape[-1]
        w_cols = wqk_ref[:, start:start + width] if wi == 0 else w_ref[:, n_qk + start:n_qk + start + width]
        acc = jnp.dot(xs_ref[...], w_cols, preferred_element_type=F32)
        if scale is None or scale == "swish":
            if scale == "swish":
                half = 0.5 * acc
                acc = half + half * jnp.tanh(half)
            for i in range(width // sw):
                o_ref[0, slab0 + i] = acc[:, i * sw:(i + 1) * sw].astype(o_ref.dtype)
            continue
        if scale not in scaled:
            scaled[scale] = (cos * scale, sin * scale)
        c, s = scaled[scale]
        if mode == "ret":
            x1, x2 = acc[:, :LANES], acc[:, LANES:]
            o_ref[0, slab0, :, 0:LANES] = (x1 * c - x2 * s).astype(o_ref.dtype)
            o_ref[0, slab0, :, LANES:width] = (x1 * s + x2 * c).astype(o_ref.dtype)
        else:
            for i in range(width // sw):
                xh = acc[:, i * sw:(i + 1) * sw]
                y = xh * c + pltpu.roll(xh, LANES // 2, 1) * s
                o_ref[0, slab0 + i] = y.astype(o_ref.dtype)


def _inproj(x2d, gain, w, tabs, *, mode, batch, seq, tm=1024):
    m_total, d = x2d.shape
    n_qk = 2 * (RET_HEADS * RET_DK if mode == "ret" else MOBA_HEADS * MOBA_DH)
    tiles_per_seq = seq // tm
    tab_spec = pl.BlockSpec((tm, LANES), lambda m: (m % tiles_per_seq, 0))
    outs = _inproj_outputs(mode)
    return pl.pallas_call(
        functools.partial(_inproj_kernel, mode=mode),
        grid=(m_total // tm,),
        in_specs=[pl.BlockSpec((tm, d), lambda m: (m, 0)), _resident((1, d)), _resident(w.shape),
                  tab_spec, tab_spec],
        out_specs=[pl.BlockSpec((1, ns, tm, sw), lambda m: (m // tiles_per_seq, 0, m % tiles_per_seq, 0))
                   for ns, sw in outs],
        out_shape=[jax.ShapeDtypeStruct((batch, ns, seq, sw), BF16) for ns, sw in outs],
        scratch_shapes=[pltpu.VMEM((tm, d), BF16), pltpu.VMEM((d, n_qk), BF16)],
        compiler_params=pltpu.CompilerParams(
            dimension_semantics=("arbitrary",), vmem_limit_bytes=VMEM_LIMIT),
        name=f"inproj_{mode}",
    )(x2d, gain, w, *tabs)


def _retention_kernel(lg_ref, q_ref, k_ref, v_ref, g_ref, o_ref, state_ref):
    L = RET_CHUNK
    seq = q_ref.shape[0]
    lg = lg_ref[pl.program_id(1)]

    ii = lax.broadcasted_iota(jnp.int32, (L, L), 0)
    jj = lax.broadcasted_iota(jnp.int32, (L, L), 1)
    diff = (ii - jj).astype(F32)
    d_intra = jnp.where(diff >= 0, jnp.exp(lg * jnp.maximum(diff, 0.0)), 0.0)
    col = lax.broadcasted_iota(jnp.int32, (L, 1), 0).astype(F32)
    q_decay = jnp.exp(lg * (col + 1.0))
    row = lax.broadcasted_iota(jnp.int32, (1, L), 1).astype(F32)
    k_decay = jnp.exp(lg * (L - 1.0 - row))
    chunk_decay = jnp.exp(jnp.full((1, 1), lg * L, F32))

    state_ref[...] = jnp.zeros_like(state_ref)
    for c in range(seq // L):
        rows = pl.ds(c * L, L)
        qc = q_ref[rows, :]
        kc = k_ref[rows, :]
        vc = v_ref[rows, :]
        scores = lax.dot_general(qc, kc, NT_DIMS, preferred_element_type=F32) * d_intra
        o = jnp.dot(scores.astype(BF16), vc, preferred_element_type=F32)
        state = state_ref[...]
        o = o + jnp.dot(qc, state.astype(BF16), preferred_element_type=F32) * q_decay
        kd_t = (kc.astype(F32).T * k_decay).astype(BF16)
        state_ref[...] = state * chunk_decay + jnp.dot(kd_t, vc, preferred_element_type=F32)
        o = o * lax.rsqrt(jnp.mean(o * o, axis=-1, keepdims=True) + NORM_EPS)
        o_ref[rows, :] = (o * g_ref[rows, :].astype(F32)).astype(o_ref.dtype)


def _retention(qk, vg, log_gamma, batch, seq):
    H, DK, DV = RET_HEADS, RET_DK, RET_DV
    grid_spec = pltpu.PrefetchScalarGridSpec(
        num_scalar_prefetch=1,
        grid=(batch, H),
        in_specs=[
            pl.BlockSpec((None, None, seq, DK), lambda b, h, lg: (b, h, 0, 0)),
            pl.BlockSpec((None, None, seq, DK), lambda b, h, lg: (b, H + h, 0, 0)),
            pl.BlockSpec((None, None, seq, DV), lambda b, h, lg: (b, h, 0, 0)),
            pl.BlockSpec((None, None, seq, DV), lambda b, h, lg: (b, H + h, 0, 0)),
        ],
        out_specs=pl.BlockSpec((None, seq, DV), lambda b, h, lg: (b, 0, h)),
        scratch_shapes=[pltpu.VMEM((DK, DV), F32)],
    )
    return pl.pallas_call(
        _retention_kernel,
        grid_spec=grid_spec,
        out_shape=jax.ShapeDtypeStruct((batch, seq, H * DV), BF16),
        compiler_params=pltpu.CompilerParams(
            dimension_semantics=("arbitrary", "arbitrary"), vmem_limit_bytes=VMEM_LIMIT),
        name="retention",
    )(log_gamma, qk, qk, vg, vg)


MOBA_HEADS_PER_STEP = 2
MOBA_SCRATCH_PER_HEAD = 7


def _moba_kernel(q_ref, k_ref, v_ref, o_ref, *scratch):
    dh = q_ref.shape[2]
    for i in range(q_ref.shape[0]):
        bufs = scratch[i * MOBA_SCRATCH_PER_HEAD:(i + 1) * MOBA_SCRATCH_PER_HEAD]
        _moba_head(q_ref.at[i], k_ref.at[i], v_ref.at[i], o_ref, i * dh, *bufs)


def _moba_head(q_ref, k_ref, v_ref, o_ref, col0, ka_ref, ksum_ref, vt_ref, s0_ref, s1_ref, p0_ref, p1_ref):
    s_bufs, p_bufs = (s0_ref, s1_ref), (p0_ref, p1_ref)
    BS = MOBA_BLOCK
    seq, dh = q_ref.shape
    nb = seq // BS
    G = ka_ref.shape[0] - seq

    for j in range(nb):
        ksum = jnp.sum(k_ref[pl.ds(j * BS, BS), :].astype(F32), axis=0, keepdims=True)
        hi = ksum.astype(BF16).astype(F32)
        ksum_ref[j:j + 1, :] = hi
        ksum_ref[nb + j:nb + j + 1, :] = ksum - hi
    ka_ref[0:G, :] = ksum_ref[...].astype(BF16)
    ka_ref[G:, :] = k_ref[...]
    gate_floor = NEG * BS * LOG2E

    QT = 2
    order = list(range(nb // QT - 1, -1, -1))

    def scores(i):
        t = order[i]
        nr = G + (t + 1) * QT * BS
        s_bufs[i % 2][0:nr, :] = lax.dot_general(ka_ref[0:nr, :], q_ref[pl.ds(t * QT * BS, QT * BS), :],
                                                 NT_DIMS, preferred_element_type=F32)

    scores(0)

    for j in range(nb):
        cols = pl.ds(j * BS, BS)
        vt_ref[0:dh, cols] = v_ref[cols, :].astype(F32).T.astype(BF16)
    vt_ref[dh:, :] = jnp.ones((vt_ref.shape[0] - dh, seq), BF16)

    key_idx = lax.broadcasted_iota(jnp.int32, (BS, BS), 0)
    qry_idx = lax.broadcasted_iota(jnp.int32, (BS, BS), 1)
    causal = key_idx <= qry_idx

    for i, t in enumerate(order):
        nk = (t + 1) * QT * BS
        if i + 1 < len(order):
            scores(i + 1)
        s_cur, p_cur = s_bufs[i % 2], p_bufs[i % 2]

        for hf in range(QT):
            n = t * QT + hf
            cs = pl.ds(hf * BS, BS)

            gate = [s_cur[j:j + 1, cs] + s_cur[nb + j:nb + j + 1, cs] for j in range(n)]

            sels = []
            for j in range(n):
                rank = jnp.where(gate[j] < gate_floor, float(nb - n), 0.0)
                for r in range(n):
                    if r != j:
                        ahead = (gate[r] >= gate[j]) if r < j else (gate[r] > gate[j])
                        rank = rank + ahead.astype(F32)
                sels.append(rank < MOBA_TOPK)

            s_own = jnp.where(causal, s_cur[pl.ds(G + n * BS, BS), cs], NEG)
            m = jnp.max(s_own, axis=0, keepdims=True)
            for j in range(n):
                m_blk = jnp.max(s_cur[pl.ds(G + j * BS, BS), cs], axis=0, keepdims=True)
                m = jnp.maximum(m, jnp.where(sels[j], m_blk, NEG))

            p_cur[pl.ds(n * BS, BS), cs] = jnp.exp2(s_own - m).astype(BF16)
            for j in range(n):
                p = jnp.exp2(s_cur[pl.ds(G + j * BS, BS), cs] - jnp.where(sels[j], m, -NEG))
                p_cur[pl.ds(j * BS, BS), cs] = p.astype(BF16)
            for j in range(n + 1, (t + 1) * QT):
                p_cur[pl.ds(j * BS, BS), cs] = jnp.zeros((BS, BS), BF16)

        acc = jnp.dot(vt_ref[:, 0:nk], p_cur[0:nk, :], preferred_element_type=F32)
        for hf in range(QT):
            cs = slice(hf * BS, (hf + 1) * BS)
            out = acc[0:dh, cs] * (1.0 / acc[dh:dh + 1, cs])
            o_ref[pl.ds((t * QT + hf) * BS, BS), col0:col0 + dh] = out.T.astype(o_ref.dtype)


def _moba(qkv, batch, seq):
    H, DH, HP = MOBA_HEADS, MOBA_DH, MOBA_HEADS_PER_STEP
    assert 2 * (seq // MOBA_BLOCK) == BF16_SUBLANES, "hi and lo key-sum rows must fill one bf16 tile"
    groups = H // HP
    per_head = [pltpu.VMEM((BF16_SUBLANES + seq, DH), BF16),
                pltpu.VMEM((BF16_SUBLANES, DH), F32),
                pltpu.VMEM((DH + BF16_SUBLANES, seq), BF16),
                pltpu.VMEM((BF16_SUBLANES + seq, 2 * MOBA_BLOCK), F32),
                pltpu.VMEM((BF16_SUBLANES + seq, 2 * MOBA_BLOCK), F32),
                pltpu.VMEM((seq, 2 * MOBA_BLOCK), BF16), pltpu.VMEM((seq, 2 * MOBA_BLOCK), BF16)]
    assert len(per_head) == MOBA_SCRATCH_PER_HEAD
    return pl.pallas_call(
        _moba_kernel,
        grid=(batch, groups),
        in_specs=[
            pl.BlockSpec((None, HP, seq, DH), lambda b, g: (b, g, 0, 0)),
            pl.BlockSpec((None, HP, seq, DH), lambda b, g: (b, groups + g, 0, 0)),
            pl.BlockSpec((None, HP, seq, DH), lambda b, g: (b, 2 * groups + g, 0, 0)),
        ],
        out_specs=pl.BlockSpec((None, seq, HP * DH), lambda b, g: (b, 0, g)),
        out_shape=jax.ShapeDtypeStruct((batch, seq, H * DH), BF16),
        scratch_shapes=per_head * HP,
        compiler_params=pltpu.CompilerParams(
            dimension_semantics=("arbitrary", "arbitrary"), vmem_limit_bytes=VMEM_LIMIT),
        name="moba",
    )(qkv, qkv, qkv)


def _outproj_mlp_kernel(a_ref, h_ref, wo_ref, gpost_ref, gpre_ref, gmpost_ref, wup_ref, wdn_ref,
                        o_ref, xn_ref, hid_ref, *, sub, ff_chunk):
    assert sum(sub) == a_ref.shape[0]
    subs = [pl.ds(sum(sub[:i]), sub[i]) for i in range(len(sub))]
    for rows in subs:
        mix = jnp.dot(a_ref[rows, :], wo_ref[...], preferred_element_type=F32)
        h1 = h_ref[rows, :] + _rms_norm(mix, gpost_ref[...])
        o_ref[rows, :] = h1
        xn_ref[rows, :] = _rms_norm(h1, gpre_ref[...]).astype(BF16)
    for rows in subs:
        for j in range(wup_ref.shape[1] // ff_chunk):
            cols = pl.ds(j * ff_chunk, ff_chunk)
            u = jnp.maximum(jnp.dot(xn_ref[rows, :], wup_ref[:, cols], preferred_element_type=F32), 0.0)
            hid_ref[rows, cols] = (u * u).astype(BF16)
    for rows in subs:
        dn = jnp.dot(hid_ref[rows, :], wdn_ref[...], preferred_element_type=F32)
        o_ref[rows, :] = o_ref[rows, :] + _rms_norm(dn, gmpost_ref[...])


def _outproj_mlp(a2d, h2d, w_out, g_post, g_mpre, g_mpost, w_up, w_down, *, tm=1024, sub=(256, 256, 256, 256), ff_chunk=1024):
    m_total, d = h2d.shape
    ka = a2d.shape[1]
    dff = w_up.shape[1]
    return pl.pallas_call(
        functools.partial(_outproj_mlp_kernel, sub=sub, ff_chunk=ff_chunk),
        grid=(m_total // tm,),
        in_specs=[
            pl.BlockSpec((tm, ka), lambda m: (m, 0)),
            pl.BlockSpec((tm, d), lambda m: (m, 0)),
            _resident((ka, d)),
            _resident((1, d)),
            _resident((1, d)),
            _resident((1, d)),
            _resident((d, dff)),
            _resident((dff, d)),
        ],
        out_specs=pl.BlockSpec((tm, d), lambda m: (m, 0)),
        out_shape=jax.ShapeDtypeStruct((m_total, d), F32),
        scratch_shapes=[pltpu.VMEM((tm, d), BF16), pltpu.VMEM((tm, dff), BF16)],
        compiler_params=pltpu.CompilerParams(
            dimension_semantics=("arbitrary",), vmem_limit_bytes=VMEM_LIMIT),
        name=f"outproj_mlp_k{ka}",
    )(a2d, h2d, w_out, g_post, g_mpre, g_mpost, w_up, w_down)


def _ret_tables(seq):
    freq = 1.0 / (RET_THETA ** jnp.linspace(0.0, 1.0, RET_DK // 2, dtype=F32))
    ang = jnp.arange(seq)[:, None].astype(F32) * freq[None, :]
    return jnp.cos(ang), jnp.sin(ang)


def _moba_tables(seq):
    half = ROPE_DIM // 2
    inv = ROPE_THETA ** (-jnp.arange(0, ROPE_DIM, 2, dtype=F32) / ROPE_DIM)
    ang = jnp.arange(seq)[:, None].astype(F32) * inv[None, :]
    cos, sin = jnp.cos(ang), jnp.sin(ang)
    gap = LANES // 2 - half
    c = jnp.concatenate([cos, jnp.ones((seq, gap), F32), cos, jnp.ones((seq, gap), F32)], axis=1)
    s = jnp.concatenate([-sin, jnp.zeros((seq, gap), F32), sin, jnp.zeros((seq, gap), F32)], axis=1)
    return c, s


def kernel(x, ln_mix_pre_0, ln_mix_post_0, w_in_0, w_out_0, ln_mlp_pre_0, ln_mlp_post_0, w_up_0, w_down_0,
           ln_mix_pre_1, ln_mix_post_1, w_in_1, w_out_1, ln_mlp_pre_1, ln_mlp_post_1, w_up_1, w_down_1):
    batch, seq, d = x.shape
    row = lambda g: g.reshape(1, d).astype(F32)

    log_gamma = jnp.log(1.0 - 2.0 ** (-5.0 - jnp.arange(RET_HEADS, dtype=F32)))

    h = x.reshape(batch * seq, d)

    qk, vg = _inproj(h, row(ln_mix_pre_0), w_in_0.astype(BF16), _ret_tables(seq), mode="ret", batch=batch, seq=seq)
    mix = _retention(qk, vg, log_gamma, batch, seq)
    h = _outproj_mlp(mix.reshape(batch * seq, -1), h, w_out_0.astype(BF16), row(ln_mix_post_0),
                     row(ln_mlp_pre_0), row(ln_mlp_post_0), w_up_0.astype(BF16), w_down_0.astype(BF16))

    (qkv,) = _inproj(h, row(ln_mix_pre_1), w_in_1.astype(BF16), _moba_tables(seq), mode="moba", batch=batch,
                     seq=seq)
    mix = _moba(qkv, batch, seq)
    h = _outproj_mlp(mix.reshape(batch * seq, -1), h, w_out_1.astype(BF16), row(ln_mix_post_1),
                     row(ln_mlp_pre_1), row(ln_mlp_post_1), w_up_1.astype(BF16), w_down_1.astype(BF16))
    return h.reshape(batch, seq, d)
```

```python
import functools
import math

import jax
import jax.numpy as jnp
from jax import lax
from jax.experimental import pallas as pl
from jax.experimental.pallas import tpu as pltpu

D_MODEL = 1024
D_FF = 4 * D_MODEL
NORM_EPS = 1e-6
NEG = -1e30
LOG2E = math.log2(math.e)

RET_HEADS = 4
RET_DK = 256
RET_DV = 512
RET_THETA = 10000.0
RET_CHUNK = 256

MOBA_HEADS = 8
MOBA_DH = 128
MOBA_BLOCK = 256
MOBA_TOPK = 3
ROPE_THETA = 500000.0
ROPE_DIM = 32

LANES = 128
BF16_SUBLANES = 16
VMEM_LIMIT = 60 * 1024 * 1024

F32 = jnp.float32
BF16 = jnp.bfloat16
NT_DIMS = (((1,), (1,)), ((), ()))


def _rms_norm(x, g):
    return x * lax.rsqrt(jnp.mean(x * x, axis=-1, keepdims=True) + NORM_EPS) * g


def _resident(shape):
    return pl.BlockSpec(shape, lambda *_: (0,) * len(shape), pipeline_mode=pl.Buffered(1))


def _inproj_outputs(mode):
    if mode == "ret":
        return [(2 * RET_HEADS, RET_DK), (2 * RET_HEADS, RET_DV)]
    return [(3 * MOBA_HEADS, MOBA_DH)]


def _inproj_plan(mode):
    if mode == "ret":
        hq = RET_HEADS * RET_DK
        hv = RET_HEADS * RET_DV
        plan = [(1, hv + c * RET_DV, RET_DV, "swish", 1, RET_HEADS + c) for c in range(RET_HEADS)]
        plan += [(0, h * RET_DK, RET_DK, 1.0, 0, h) for h in range(RET_HEADS)]
        plan += [(0, hq + h * RET_DK, RET_DK, RET_DK ** -0.5, 0, RET_HEADS + h) for h in range(RET_HEADS)]
        plan += [(1, c * RET_DV, RET_DV, None, 1, c) for c in range(RET_HEADS)]
    else:
        hq = MOBA_HEADS * MOBA_DH
        per = 4
        cw = per * MOBA_DH
        plan = [(0, c * cw, cw, MOBA_DH ** -0.5 * LOG2E, 0, c * per) for c in range(hq // cw)]
        plan += [(0, hq + c * cw, cw, 1.0, 0, MOBA_HEADS + c * per) for c in range(hq // cw)]
        plan += [(1, c * cw, cw, None, 0, 2 * MOBA_HEADS + c * per) for c in range(hq // cw)]
    return plan


PERM_W = 256


def _qk_column_permutation(mode):
    new = lax.broadcasted_iota(jnp.int32, (PERM_W, PERM_W), 1)
    old = lax.broadcasted_iota(jnp.int32, (PERM_W, PERM_W), 0)
    if mode == "ret":
        half = RET_DK // 2
        src = jnp.where(new < half, 2 * new, 2 * (new - half) + 1)
    else:
        j = new % MOBA_DH
        h16, mid = ROPE_DIM // 2, MOBA_DH // 2
        src_j = jnp.where(j < h16, j, jnp.where(j < mid, j + h16, jnp.where(j < mid + h16, j - (mid - h16), j)))
        src = (new - j) + src_j
    return (old == src).astype(BF16)


def _inproj_kernel(x_ref, g_ref, w_ref, cos_ref, sin_ref, *rest, mode):
    out_refs, xs_ref, wqk_ref = rest[:-2], rest[-2], rest[-1]
    n_qk = wqk_ref.shape[1]

    @pl.when(pl.program_id(0) == 0)
    def _():
        perm = _qk_column_permutation(mode)
        for c0 in range(0, n_qk, PERM_W):
            cols = slice(c0, c0 + PERM_W)
            wqk_ref[:, cols] = jnp.dot(w_ref[:, cols], perm, preferred_element_type=F32).astype(BF16)

    xs_ref[...] = _rms_norm(x_ref[...], g_ref[...]).astype(BF16)
    cos, sin = cos_ref[...], sin_ref[...]
    scaled = {}
    for wi, start, width, scale, oi, slab0 in _inproj_plan(mode):
        o_ref = out_refs[oi]
        sw = o_ref.shape[-1]
        w_cols = wqk_ref[:, start:start + width] if wi == 0 else w_ref[:, n_qk + start:n_qk + start + width]
        acc = jnp.dot(xs_ref[...], w_cols, preferred_element_type=F32)
        if scale is None or scale == "swish":
            if scale == "swish":
                half = 0.5 * acc
                acc = half + half * jnp.tanh(half)
            for i in range(width // sw):
                o_ref[0, slab0 + i] = acc[:, i * sw:(i + 1) * sw].astype(o_ref.dtype)
            continue
        if scale not in scaled:
            scaled[scale] = (cos * scale, sin * scale)
        c, s = scaled[scale]
        if mode == "ret":
            x1, x2 = acc[:, :LANES], acc[:, LANES:]
            o_ref[0, slab0, :, 0:LANES] = (x1 * c - x2 * s).astype(o_ref.dtype)
            o_ref[0, slab0, :, LANES:width] = (x1 * s + x2 * c).astype(o_ref.dtype)
        else:
            for i in range(width // sw):
                xh = acc[:, i * sw:(i + 1) * sw]
                y = xh * c + pltpu.roll(xh, LANES // 2, 1) * s
                o_ref[0, slab0 + i] = y.astype(o_ref.dtype)


def _inproj(x2d, gain, w, tabs, *, mode, batch, seq, tm=1024):
    m_total, d = x2d.shape
    n_qk = 2 * (RET_HEADS * RET_DK if mode == "ret" else MOBA_HEADS * MOBA_DH)
    tiles_per_seq = seq // tm
    tab_spec = pl.BlockSpec((tm, LANES), lambda m: (m % tiles_per_seq, 0))
    outs = _inproj_outputs(mode)
    return pl.pallas_call(
        functools.partial(_inproj_kernel, mode=mode),
        grid=(m_total // tm,),
        in_specs=[pl.BlockSpec((tm, d), lambda m: (m, 0)), _resident((1, d)), _resident(w.shape),
                  tab_spec, tab_spec],
        out_specs=[pl.BlockSpec((1, ns, tm, sw), lambda m: (m // tiles_per_seq, 0, m % tiles_per_seq, 0))
                   for ns, sw in outs],
        out_shape=[jax.ShapeDtypeStruct((batch, ns, seq, sw), BF16) for ns, sw in outs],
        scratch_shapes=[pltpu.VMEM((tm, d), BF16), pltpu.VMEM((d, n_qk), BF16)],
        compiler_params=pltpu.CompilerParams(
            dimension_semantics=("arbitrary",), vmem_limit_bytes=VMEM_LIMIT),
        name=f"inproj_{mode}",
    )(x2d, gain, w, *tabs)


RET_HEADS_PER_STEP = 2


def _retention_decays(lg):
    L = RET_CHUNK
    ii = lax.broadcasted_iota(jnp.int32, (L, L), 0)
    jj = lax.broadcasted_iota(jnp.int32, (L, L), 1)
    diff = (ii - jj).astype(F32)
    d_intra = jnp.where(diff >= 0, jnp.exp(lg * jnp.maximum(diff, 0.0)), 0.0)
    col = lax.broadcasted_iota(jnp.int32, (L, 1), 0).astype(F32)
    q_decay = jnp.exp(lg * (col + 1.0))
    row = lax.broadcasted_iota(jnp.int32, (1, L), 1).astype(F32)
    k_decay = jnp.exp(lg * (L - 1.0 - row))
    chunk_decay = jnp.exp(jnp.full((1, 1), lg * L, F32))
    return d_intra, q_decay, k_decay, chunk_decay


def _retention_kernel(lg_ref, q_ref, k_ref, v_ref, g_ref, o_ref, state_ref):
    L = RET_CHUNK
    hp, seq, _ = q_ref.shape
    dv = v_ref.shape[2]
    head0 = pl.program_id(1) * hp
    decays = [_retention_decays(lg_ref[head0 + i]) for i in range(hp)]

    state_ref[...] = jnp.zeros_like(state_ref)
    for c in range(seq // L):
        rows = pl.ds(c * L, L)
        for i in range(hp):
            d_intra, q_decay, k_decay, chunk_decay = decays[i]
            qc = q_ref[i, rows, :]
            kc = k_ref[i, rows, :]
            vc = v_ref[i, rows, :]
            scores = lax.dot_general(qc, kc, NT_DIMS, preferred_element_type=F32) * d_intra
            o = jnp.dot(scores.astype(BF16), vc, preferred_element_type=F32)
            state = state_ref[i]
            o = o + jnp.dot(qc, state.astype(BF16), preferred_element_type=F32) * q_decay
            kd_t = (kc.astype(F32).T * k_decay).astype(BF16)
            state_ref[i] = state * chunk_decay + jnp.dot(kd_t, vc, preferred_element_type=F32)
            o = o * lax.rsqrt(jnp.mean(o * o, axis=-1, keepdims=True) + NORM_EPS)
            o_ref[rows, i * dv:(i + 1) * dv] = (o * g_ref[i, rows, :].astype(F32)).astype(o_ref.dtype)


def _retention(qk, vg, log_gamma, batch, seq):
    H, DK, DV, HP = RET_HEADS, RET_DK, RET_DV, RET_HEADS_PER_STEP
    groups = H // HP
    grid_spec = pltpu.PrefetchScalarGridSpec(
        num_scalar_prefetch=1,
        grid=(batch, groups),
        in_specs=[
            pl.BlockSpec((None, HP, seq, DK), lambda b, g, lg: (b, g, 0, 0)),
            pl.BlockSpec((None, HP, seq, DK), lambda b, g, lg: (b, groups + g, 0, 0)),
            pl.BlockSpec((None, HP, seq, DV), lambda b, g, lg: (b, g, 0, 0)),
            pl.BlockSpec((None, HP, seq, DV), lambda b, g, lg: (b, groups + g, 0, 0)),
        ],
        out_specs=pl.BlockSpec((None, seq, HP * DV), lambda b, g, lg: (b, 0, g)),
        scratch_shapes=[pltpu.VMEM((HP, DK, DV), F32)],
    )
    return pl.pallas_call(
        _retention_kernel,
        grid_spec=grid_spec,
        out_shape=jax.ShapeDtypeStruct((batch, seq, H * DV), BF16),
        compiler_params=pltpu.CompilerParams(
            dimension_semantics=("arbitrary", "arbitrary"), vmem_limit_bytes=VMEM_LIMIT),
        name="retention",
    )(log_gamma, qk, qk, vg, vg)


MOBA_HEADS_PER_STEP = 2
MOBA_SCRATCH_PER_HEAD = 7


def _moba_kernel(q_ref, k_ref, v_ref, o_ref, *scratch):
    dh = q_ref.shape[2]
    for i in range(q_ref.shape[0]):
        bufs = scratch[i * MOBA_SCRATCH_PER_HEAD:(i + 1) * MOBA_SCRATCH_PER_HEAD]
        _moba_head(q_ref.at[i], k_ref.at[i], v_ref.at[i], o_ref, i * dh, *bufs)


def _moba_head(q_ref, k_ref, v_ref, o_ref, col0, ka_ref, ksum_ref, vt_ref, s0_ref, s1_ref, p0_ref, p1_ref):
    s_bufs, p_bufs = (s0_ref, s1_ref), (p0_ref, p1_ref)
    BS = MOBA_BLOCK
    seq, dh = q_ref.shape
    nb = seq // BS
    G = ka_ref.shape[0] - seq

    for j in range(nb):
        ksum = jnp.sum(k_ref[pl.ds(j * BS, BS), :].astype(F32), axis=0, keepdims=True)
        hi = ksum.astype(BF16).astype(F32)
        ksum_ref[j:j + 1, :] = hi
        ksum_ref[nb + j:nb + j + 1, :] = ksum - hi
    ka_ref[0:G, :] = ksum_ref[...].astype(BF16)
    ka_ref[G:, :] = k_ref[...]
    gate_floor = NEG * BS * LOG2E

    QT = 2
    order = list(range(nb // QT - 1, -1, -1))

    def scores(i):
        t = order[i]
        nr = G + (t + 1) * QT * BS
        s_bufs[i % 2][0:nr, :] = lax.dot_general(ka_ref[0:nr, :], q_ref[pl.ds(t * QT * BS, QT * BS), :],
                                                 NT_DIMS, preferred_element_type=F32)

    scores(0)

    for j in range(nb):
        cols = pl.ds(j * BS, BS)
        vt_ref[0:dh, cols] = v_ref[cols, :].astype(F32).T.astype(BF16)
    vt_ref[dh:, :] = jnp.ones((vt_ref.shape[0] - dh, seq), BF16)

    key_idx = lax.broadcasted_iota(jnp.int32, (BS, BS), 0)
    qry_idx = lax.broadcasted_iota(jnp.int32, (BS, BS), 1)
    causal = key_idx <= qry_idx

    for i, t in enumerate(order):
        nk = (t + 1) * QT * BS
        if i + 1 < len(order):
            scores(i + 1)
        s_cur, p_cur = s_bufs[i % 2], p_bufs[i % 2]

        for hf in range(QT):
            n = t * QT + hf
            cs = pl.ds(hf * BS, BS)

            gate = [s_cur[j:j + 1, cs] + s_cur[nb + j:nb + j + 1, cs] for j in range(n)]

            sels = []
            for j in range(n):
                rank = jnp.where(gate[j] < gate_floor, float(nb - n), 0.0)
                for r in range(n):
                    if r != j:
                        ahead = (gate[r] >= gate[j]) if r < j else (gate[r] > gate[j])
                        rank = rank + ahead.astype(F32)
                sels.append(rank < MOBA_TOPK)

            s_own = jnp.where(causal, s_cur[pl.ds(G + n * BS, BS), cs], NEG)
            m = jnp.max(s_own, axis=0, keepdims=True)
            for j in range(n):
                m_blk = jnp.max(s_cur[pl.ds(G + j * BS, BS), cs], axis=0, keepdims=True)
                m = jnp.maximum(m, jnp.where(sels[j], m_blk, NEG))

            p_cur[pl.ds(n * BS, BS), cs] = jnp.exp2(s_own - m).astype(BF16)
            for j in range(n):
                p = jnp.exp2(s_cur[pl.ds(G + j * BS, BS), cs] - jnp.where(sels[j], m, -NEG))
                p_cur[pl.ds(j * BS, BS), cs] = p.astype(BF16)
            for j in range(n + 1, (t + 1) * QT):
                p_cur[pl.ds(j * BS, BS), cs] = jnp.zeros((BS, BS), BF16)

        acc = jnp.dot(vt_ref[:, 0:nk], p_cur[0:nk, :], preferred_element_type=F32)
        for hf in range(QT):
            cs = slice(hf * BS, (hf + 1) * BS)
            out = acc[0:dh, cs] * (1.0 / acc[dh:dh + 1, cs])
            o_ref[pl.ds((t * QT + hf) * BS, BS), col0:col0 + dh] = out.T.astype(o_ref.dtype)


def _moba(qkv, batch, seq):
    H, DH, HP = MOBA_HEADS, MOBA_DH, MOBA_HEADS_PER_STEP
    assert 2 * (seq // MOBA_BLOCK) == BF16_SUBLANES, "hi and lo key-sum rows must fill one bf16 tile"
    groups = H // HP
    per_head = [pltpu.VMEM((BF16_SUBLANES + seq, DH), BF16),
                pltpu.VMEM((BF16_SUBLANES, DH), F32),
                pltpu.VMEM((DH + BF16_SUBLANES, seq), BF16),
                pltpu.VMEM((BF16_SUBLANES + seq, 2 * MOBA_BLOCK), F32),
                pltpu.VMEM((BF16_SUBLANES + seq, 2 * MOBA_BLOCK), F32),
                pltpu.VMEM((seq, 2 * MOBA_BLOCK), BF16), pltpu.VMEM((seq, 2 * MOBA_BLOCK), BF16)]
    assert len(per_head) == MOBA_SCRATCH_PER_HEAD
    return pl.pallas_call(
        _moba_kernel,
        grid=(batch, groups),
        in_specs=[
            pl.BlockSpec((None, HP, seq, DH), lambda b, g: (b, g, 0, 0)),
            pl.BlockSpec((None, HP, seq, DH), lambda b, g: (b, groups + g, 0, 0)),
            pl.BlockSpec((None, HP, seq, DH), lambda b, g: (b, 2 * groups + g, 0, 0)),
        ],
        out_specs=pl.BlockSpec((None, seq, HP * DH), lambda b, g: (b, 0, g)),
        out_shape=jax.ShapeDtypeStruct((batch, seq, H * DH), BF16),
        scratch_shapes=per_head * HP,
        compiler_params=pltpu.CompilerParams(
            dimension_semantics=("arbitrary", "arbitrary"), vmem_limit_bytes=VMEM_LIMIT),
        name="moba",
    )(qkv, qkv, qkv)


def _outproj_mlp_kernel(a_ref, h_ref, wo_ref, gpost_ref, gpre_ref, gmpost_ref, wup_ref, wdn_ref,
                        o_ref, xn_ref, hid_ref, *, sub, ff_chunk):
    assert sum(sub) == a_ref.shape[0]
    subs = [pl.ds(sum(sub[:i]), sub[i]) for i in range(len(sub))]
    for rows in subs:
        mix = jnp.dot(a_ref[rows, :], wo_ref[...], preferred_element_type=F32)
        h1 = h_ref[rows, :] + _rms_norm(mix, gpost_ref[...])
        o_ref[rows, :] = h1
        xn_ref[rows, :] = _rms_norm(h1, gpre_ref[...]).astype(BF16)
    for rows in subs:
        for j in range(wup_ref.shape[1] // ff_chunk):
            cols = pl.ds(j * ff_chunk, ff_chunk)
            u = jnp.maximum(jnp.dot(xn_ref[rows, :], wup_ref[:, cols], preferred_element_type=F32), 0.0)
            hid_ref[rows, cols] = (u * u).astype(BF16)
    for rows in subs:
        dn = jnp.dot(hid_ref[rows, :], wdn_ref[...], preferred_element_type=F32)
        o_ref[rows, :] = o_ref[rows, :] + _rms_norm(dn, gmpost_ref[...])


def _outproj_mlp(a2d, h2d, w_out, g_post, g_mpre, g_mpost, w_up, w_down, *, tm=1024, sub=(256, 256, 256, 256), ff_chunk=1024):
    m_total, d = h2d.shape
    ka = a2d.shape[1]
    dff = w_up.shape[1]
    return pl.pallas_call(
        functools.partial(_outproj_mlp_kernel, sub=sub, ff_chunk=ff_chunk),
        grid=(m_total // tm,),
        in_specs=[
            pl.BlockSpec((tm, ka), lambda m: (m, 0)),
            pl.BlockSpec((tm, d), lambda m: (m, 0)),
            _resident((ka, d)),
            _resident((1, d)),
            _resident((1, d)),
            _resident((1, d)),
            _resident((d, dff)),
            _resident((dff, d)),
        ],
        out_specs=pl.BlockSpec((tm, d), lambda m: (m, 0)),
        out_shape=jax.ShapeDtypeStruct((m_total, d), F32),
        scratch_shapes=[pltpu.VMEM((tm, d), BF16), pltpu.VMEM((tm, dff), BF16)],
        compiler_params=pltpu.CompilerParams(
            dimension_semantics=("arbitrary",), vmem_limit_bytes=VMEM_LIMIT),
        name=f"outproj_mlp_k{ka}",
    )(a2d, h2d, w_out, g_post, g_mpre, g_mpost, w_up, w_down)


def _ret_tables(seq):
    freq = 1.0 / (RET_THETA ** jnp.linspace(0.0, 1.0, RET_DK // 2, dtype=F32))
    ang = jnp.arange(seq)[:, None].astype(F32) * freq[None, :]
    return jnp.cos(ang), jnp.sin(ang)


def _moba_tables(seq):
    half = ROPE_DIM // 2
    inv = ROPE_THETA ** (-jnp.arange(0, ROPE_DIM, 2, dtype=F32) / ROPE_DIM)
    ang = jnp.arange(seq)[:, None].astype(F32) * inv[None, :]
    cos, sin = jnp.cos(ang), jnp.sin(ang)
    gap = LANES // 2 - half
    c = jnp.concatenate([cos, jnp.ones((seq, gap), F32), cos, jnp.ones((seq, gap), F32)], axis=1)
    s = jnp.concatenate([-sin, jnp.zeros((seq, gap), F32), sin, jnp.zeros((seq, gap), F32)], axis=1)
    return c, s


def kernel(x, ln_mix_pre_0, ln_mix_post_0, w_in_0, w_out_0, ln_mlp_pre_0, ln_mlp_post_0, w_up_0, w_down_0,
           ln_mix_pre_1, ln_mix_post_1, w_in_1, w_out_1, ln_mlp_pre_1, ln_mlp_post_1, w_up_1, w_down_1):
    batch, seq, d = x.shape
    row = lambda g: g.reshape(1, d).astype(F32)

    log_gamma = jnp.log(1.0 - 2.0 ** (-5.0 - jnp.arange(RET_HEADS, dtype=F32)))

    h = x.reshape(batch * seq, d)

    qk, vg = _inproj(h, row(ln_mix_pre_0), w_in_0.astype(BF16), _ret_tables(seq), mode="ret", batch=batch, seq=seq)
    mix = _retention(qk, vg, log_gamma, batch, seq)
    h = _outproj_mlp(mix.reshape(batch * seq, -1), h, w_out_0.astype(BF16), row(ln_mix_post_0),
                     row(ln_mlp_pre_0), row(ln_mlp_post_0), w_up_0.astype(BF16), w_down_0.astype(BF16))

    (qkv,) = _inproj(h, row(ln_mix_pre_1), w_in_1.astype(BF16), _moba_tables(seq), mode="moba", batch=batch,
                     seq=seq)
    mix = _moba(qkv, batch, seq)
    h = _outproj_mlp(mix.reshape(batch * seq, -1), h, w_out_1.astype(BF16), row(ln_mix_post_1),
                     row(ln_mlp_pre_1), row(ln_mlp_post_1), w_up_1.astype(BF16), w_down_1.astype(BF16))
    return h.reshape(batch, seq, d)
```

```python
import functools
import math

import jax
import jax.numpy as jnp
from jax import lax
from jax.experimental import pallas as pl
from jax.experimental.pallas import tpu as pltpu

D_MODEL = 1024
D_FF = 4 * D_MODEL
NORM_EPS = 1e-6
NEG = -1e30
LOG2E = math.log2(math.e)

RET_HEADS = 4
RET_DK = 256
RET_DV = 512
RET_THETA = 10000.0
RET_CHUNK = 256

MOBA_HEADS = 8
MOBA_DH = 128
MOBA_BLOCK = 256
MOBA_TOPK = 3
ROPE_THETA = 500000.0
ROPE_DIM = 32

LANES = 128
BF16_SUBLANES = 16
VMEM_LIMIT = 60 * 1024 * 1024

F32 = jnp.float32
BF16 = jnp.bfloat16
NT_DIMS = (((1,), (1,)), ((), ()))


def _rms_norm(x, g):
    return x * lax.rsqrt(jnp.mean(x * x, axis=-1, keepdims=True) + NORM_EPS) * g


def _resident(shape):
    return pl.BlockSpec(shape, lambda *_: (0,) * len(shape), pipeline_mode=pl.Buffered(1))


def _inproj_outputs(mode):
    if mode == "ret":
        return [(2 * RET_HEADS, RET_DK), (2 * RET_HEADS, RET_DV)]
    return [(3 * MOBA_HEADS, MOBA_DH)]


def _inproj_plan(mode):
    if mode == "ret":
        hq = RET_HEADS * RET_DK
        hv = RET_HEADS * RET_DV
        plan = [(1, hv + c * RET_DV, RET_DV, "swish", 1, RET_HEADS + c) for c in range(RET_HEADS)]
        plan += [(0, h * RET_DK, RET_DK, 1.0, 0, h) for h in range(RET_HEADS)]
        plan += [(0, hq + h * RET_DK, RET_DK, RET_DK ** -0.5, 0, RET_HEADS + h) for h in range(RET_HEADS)]
        plan += [(1, c * RET_DV, RET_DV, None, 1, c) for c in range(RET_HEADS)]
    else:
        hq = MOBA_HEADS * MOBA_DH
        per = 4
        cw = per * MOBA_DH
        plan = [(0, c * cw, cw, MOBA_DH ** -0.5 * LOG2E, 0, c * per) for c in range(hq // cw)]
        plan += [(0, hq + c * cw, cw, 1.0, 0, MOBA_HEADS + c * per) for c in range(hq // cw)]
        plan += [(1, c * cw, cw, None, 0, 2 * MOBA_HEADS + c * per) for c in range(hq // cw)]
    return plan


PERM_W = 256


def _qk_column_permutation(mode):
    new = lax.broadcasted_iota(jnp.int32, (PERM_W, PERM_W), 1)
    old = lax.broadcasted_iota(jnp.int32, (PERM_W, PERM_W), 0)
    if mode == "ret":
        half = RET_DK // 2
        src = jnp.where(new < half, 2 * new, 2 * (new - half) + 1)
    else:
        j = new % MOBA_DH
        h16, mid = ROPE_DIM // 2, MOBA_DH // 2
        src_j = jnp.where(j < h16, j, jnp.where(j < mid, j + h16, jnp.where(j < mid + h16, j - (mid - h16), j)))
        src = (new - j) + src_j
    return (old == src).astype(BF16)


def _inproj_kernel(x_ref, g_ref, w_ref, cos_ref, sin_ref, *rest, mode):
    out_refs, xs_ref, wqk_ref = rest[:-2], rest[-2], rest[-1]
    n_qk = wqk_ref.shape[1]

    @pl.when(pl.program_id(0) == 0)
    def _():
        perm = _qk_column_permutation(mode)
        for c0 in range(0, n_qk, PERM_W):
            cols = slice(c0, c0 + PERM_W)
            wqk_ref[:, cols] = jnp.dot(w_ref[:, cols], perm, preferred_element_type=F32).astype(BF16)

    xs_ref[...] = _rms_norm(x_ref[...], g_ref[...]).astype(BF16)
    cos, sin = cos_ref[...], sin_ref[...]
    scaled = {}
    for wi, start, width, scale, oi, slab0 in _inproj_plan(mode):
        o_ref = out_refs[oi]
        sw = o_ref.shape[-1]
        w_cols = wqk_ref[:, start:start + width] if wi == 0 else w_ref[:, n_qk + start:n_qk + start + width]
        acc = jnp.dot(xs_ref[...], w_cols, preferred_element_type=F32)
        if scale is None or scale == "swish":
            if scale == "swish":
                half = 0.5 * acc
                acc = half + half * jnp.tanh(half)
            for i in range(width // sw):
                o_ref[0, slab0 + i] = acc[:, i * sw:(i + 1) * sw].astype(o_ref.dtype)
            continue
        if scale not in scaled:
            scaled[scale] = (cos * scale, sin * scale)
        c, s = scaled[scale]
        if mode == "ret":
            x1, x2 = acc[:, :LANES], acc[:, LANES:]
            o_ref[0, slab0, :, 0:LANES] = (x1 * c - x2 * s).astype(o_ref.dtype)
            o_ref[0, slab0, :, LANES:width] = (x1 * s + x2 * c).astype(o_ref.dtype)
        else:
            for i in range(width // sw):
                xh = acc[:, i * sw:(i + 1) * sw]
                y = xh * c + pltpu.roll(xh, LANES // 2, 1) * s
                o_ref[0, slab0 + i] = y.astype(o_ref.dtype)


def _inproj(x2d, gain, w, tabs, *, mode, batch, seq, tm=1024):
    m_total, d = x2d.shape
    n_qk = 2 * (RET_HEADS * RET_DK if mode == "ret" else MOBA_HEADS * MOBA_DH)
    tiles_per_seq = seq // tm
    tab_spec = pl.BlockSpec((tm, LANES), lambda m: (m % tiles_per_seq, 0))
    outs = _inproj_outputs(mode)
    return pl.pallas_call(
        functools.partial(_inproj_kernel, mode=mode),
        grid=(m_total // tm,),
        in_specs=[pl.BlockSpec((tm, d), lambda m: (m, 0)), _resident((1, d)), _resident(w.shape),
                  tab_spec, tab_spec],
        out_specs=[pl.BlockSpec((1, ns, tm, sw), lambda m: (m // tiles_per_seq, 0, m % tiles_per_seq, 0))
                   for ns, sw in outs],
        out_shape=[jax.ShapeDtypeStruct((batch, ns, seq, sw), BF16) for ns, sw in outs],
        scratch_shapes=[pltpu.VMEM((tm, d), BF16), pltpu.VMEM((d, n_qk), BF16)],
        compiler_params=pltpu.CompilerParams(
            dimension_semantics=("arbitrary",), vmem_limit_bytes=VMEM_LIMIT),
        name=f"inproj_{mode}",
    )(x2d, gain, w, *tabs)


RET_HEADS_PER_STEP = 2


def _retention_decays(lg):
    L = RET_CHUNK
    ii = lax.broadcasted_iota(jnp.int32, (L, L), 0)
    jj = lax.broadcasted_iota(jnp.int32, (L, L), 1)
    diff = (ii - jj).astype(F32)
    d_intra = jnp.where(diff >= 0, jnp.exp(lg * jnp.maximum(diff, 0.0)), 0.0)
    col = lax.broadcasted_iota(jnp.int32, (L, 1), 0).astype(F32)
    q_decay = jnp.exp(lg * (col + 1.0))
    row = lax.broadcasted_iota(jnp.int32, (1, L), 1).astype(F32)
    k_decay = jnp.exp(lg * (L - 1.0 - row))
    chunk_decay = jnp.exp(jnp.full((1, 1), lg * L, F32))
    return d_intra, q_decay, k_decay, chunk_decay


def _retention_kernel(lg_ref, q_ref, k_ref, v_ref, g_ref, o_ref, state_ref):
    L = RET_CHUNK
    hp, seq, _ = q_ref.shape
    dv = v_ref.shape[2]
    head0 = pl.program_id(1) * hp
    decays = [_retention_decays(lg_ref[head0 + i]) for i in range(hp)]

    state_ref[...] = jnp.zeros_like(state_ref)
    for c in range(seq // L):
        rows = pl.ds(c * L, L)
        for i in range(hp):
            d_intra, q_decay, k_decay, chunk_decay = decays[i]
            qc = q_ref[i, rows, :]
            kc = k_ref[i, rows, :]
            vc = v_ref[i, rows, :]
            scores = lax.dot_general(qc, kc, NT_DIMS, preferred_element_type=F32) * d_intra
            o = jnp.dot(scores.astype(BF16), vc, preferred_element_type=F32)
            state = state_ref[i]
            o = o + jnp.dot(qc, state.astype(BF16), preferred_element_type=F32) * q_decay
            kd_t = (kc.astype(F32).T * k_decay).astype(BF16)
            state_ref[i] = state * chunk_decay + jnp.dot(kd_t, vc, preferred_element_type=F32)
            o = o * lax.rsqrt(jnp.mean(o * o, axis=-1, keepdims=True) + NORM_EPS)
            o_ref[rows, i * dv:(i + 1) * dv] = (o * g_ref[i, rows, :].astype(F32)).astype(o_ref.dtype)


def _retention(qk, vg, log_gamma, batch, seq):
    H, DK, DV, HP = RET_HEADS, RET_DK, RET_DV, RET_HEADS_PER_STEP
    groups = H // HP
    grid_spec = pltpu.PrefetchScalarGridSpec(
        num_scalar_prefetch=1,
        grid=(batch, groups),
        in_specs=[
            pl.BlockSpec((None, HP, seq, DK), lambda b, g, lg: (b, g, 0, 0)),
            pl.BlockSpec((None, HP, seq, DK), lambda b, g, lg: (b, groups + g, 0, 0)),
            pl.BlockSpec((None, HP, seq, DV), lambda b, g, lg: (b, g, 0, 0)),
            pl.BlockSpec((None, HP, seq, DV), lambda b, g, lg: (b, groups + g, 0, 0)),
        ],
        out_specs=pl.BlockSpec((None, seq, HP * DV), lambda b, g, lg: (b, 0, g)),
        scratch_shapes=[pltpu.VMEM((HP, DK, DV), F32)],
    )
    return pl.pallas_call(
        _retention_kernel,
        grid_spec=grid_spec,
        out_shape=jax.ShapeDtypeStruct((batch, seq, H * DV), BF16),
        compiler_params=pltpu.CompilerParams(
            dimension_semantics=("arbitrary", "arbitrary"), vmem_limit_bytes=VMEM_LIMIT),
        name="retention",
    )(log_gamma, qk, qk, vg, vg)


def _proj_retention_kernel(lg_ref, x_ref, g_ref, w_ref, cos_ref, sin_ref, o_ref,
                           xs_ref, wqk_ref, q_s, k_s, v_s, gate_s, state_ref, *, tiles_per_seq):
    H, DK, DV, L = RET_HEADS, RET_DK, RET_DV, RET_CHUNK
    step = pl.program_id(0)
    tm = x_ref.shape[0]
    n_qk = wqk_ref.shape[1]

    @pl.when(step == 0)
    def _():
        perm = _qk_column_permutation("ret")
        for c0 in range(0, n_qk, PERM_W):
            cols = slice(c0, c0 + PERM_W)
            wqk_ref[:, cols] = jnp.dot(w_ref[:, cols], perm, preferred_element_type=F32).astype(BF16)

    @pl.when(step % tiles_per_seq == 0)
    def _():
        state_ref[...] = jnp.zeros_like(state_ref)

    xs_ref[...] = _rms_norm(x_ref[...], g_ref[...]).astype(BF16)
    cos, sin = cos_ref[...], sin_ref[...]
    cos_k, sin_k = cos * DK ** -0.5, sin * DK ** -0.5

    def project(w_cols):
        return jnp.dot(xs_ref[...], w_cols, preferred_element_type=F32)

    def rotate(acc, c, s, dst):
        x1, x2 = acc[:, :LANES], acc[:, LANES:]
        dst[:, 0:LANES] = (x1 * c - x2 * s).astype(BF16)
        dst[:, LANES:] = (x1 * s + x2 * c).astype(BF16)

    for i in range(H):
        qb, kb, vb, gb = q_s.at[i % 2], k_s.at[i % 2], v_s.at[i % 2], gate_s.at[i % 2]
        half = 0.5 * project(w_ref[:, n_qk + (H + i) * DV:n_qk + (H + i + 1) * DV])
        gb[...] = (half + half * jnp.tanh(half)).astype(BF16)
        rotate(project(wqk_ref[:, i * DK:(i + 1) * DK]), cos, sin, qb)
        rotate(project(wqk_ref[:, (H + i) * DK:(H + i + 1) * DK]), cos_k, sin_k, kb)
        vb[...] = project(w_ref[:, n_qk + i * DV:n_qk + (i + 1) * DV]).astype(BF16)

        d_intra, q_decay, k_decay, chunk_decay = _retention_decays(lg_ref[i])
        for c in range(tm // L):
            rows = pl.ds(c * L, L)
            qc, kc, vc = qb[rows, :], kb[rows, :], vb[rows, :]
            scores = lax.dot_general(qc, kc, NT_DIMS, preferred_element_type=F32) * d_intra
            o = jnp.dot(scores.astype(BF16), vc, preferred_element_type=F32)
            state = state_ref[i]
            o = o + jnp.dot(qc, state.astype(BF16), preferred_element_type=F32) * q_decay
            kd_t = (kc.astype(F32).T * k_decay).astype(BF16)
            state_ref[i] = state * chunk_decay + jnp.dot(kd_t, vc, preferred_element_type=F32)
            o = o * lax.rsqrt(jnp.mean(o * o, axis=-1, keepdims=True) + NORM_EPS)
            o_ref[rows, i * DV:(i + 1) * DV] = (o * gb[rows, :].astype(F32)).astype(o_ref.dtype)


def _proj_retention(x2d, gain, w, tabs, log_gamma, *, seq, tm=1024):
    m_total, d = x2d.shape
    H, DK, DV = RET_HEADS, RET_DK, RET_DV
    tiles_per_seq = seq // tm
    const = lambda shape: pl.BlockSpec(shape, lambda m, lg: (0,) * len(shape), pipeline_mode=pl.Buffered(1))
    tab_spec = pl.BlockSpec((tm, LANES), lambda m, lg: (m % tiles_per_seq, 0))
    grid_spec = pltpu.PrefetchScalarGridSpec(
        num_scalar_prefetch=1,
        grid=(m_total // tm,),
        in_specs=[pl.BlockSpec((tm, d), lambda m, lg: (m, 0)), const((1, d)), const(w.shape), tab_spec, tab_spec],
        out_specs=pl.BlockSpec((tm, H * DV), lambda m, lg: (m, 0)),
        scratch_shapes=[pltpu.VMEM((tm, d), BF16), pltpu.VMEM((d, 2 * H * DK), BF16),
                        pltpu.VMEM((2, tm, DK), BF16), pltpu.VMEM((2, tm, DK), BF16),
                        pltpu.VMEM((2, tm, DV), BF16), pltpu.VMEM((2, tm, DV), BF16),
                        pltpu.VMEM((H, DK, DV), F32)],
    )
    return pl.pallas_call(
        functools.partial(_proj_retention_kernel, tiles_per_seq=tiles_per_seq),
        grid_spec=grid_spec,
        out_shape=jax.ShapeDtypeStruct((m_total, H * DV), BF16),
        compiler_params=pltpu.CompilerParams(
            dimension_semantics=("arbitrary",), vmem_limit_bytes=VMEM_LIMIT),
        name="proj_retention",
    )(log_gamma, x2d, gain, w, *tabs)


MOBA_HEADS_PER_STEP = 2
MOBA_SCRATCH_PER_HEAD = 7


def _moba_kernel(q_ref, k_ref, v_ref, o_ref, *scratch):
    dh = q_ref.shape[2]
    for i in range(q_ref.shape[0]):
        bufs = scratch[i * MOBA_SCRATCH_PER_HEAD:(i + 1) * MOBA_SCRATCH_PER_HEAD]
        _moba_head(q_ref.at[i], k_ref.at[i], v_ref.at[i], o_ref, i * dh, *bufs)


def _moba_head(q_ref, k_ref, v_ref, o_ref, col0, ka_ref, ksum_ref, vt_ref, s0_ref, s1_ref, p0_ref, p1_ref):
    s_bufs, p_bufs = (s0_ref, s1_ref), (p0_ref, p1_ref)
    BS = MOBA_BLOCK
    seq, dh = q_ref.shape
    nb = seq // BS
    G = ka_ref.shape[0] - seq

    for j in range(nb):
        ksum = jnp.sum(k_ref[pl.ds(j * BS, BS), :].astype(F32), axis=0, keepdims=True)
        hi = ksum.astype(BF16).astype(F32)
        ksum_ref[j:j + 1, :] = hi
        ksum_ref[nb + j:nb + j + 1, :] = ksum - hi
    ka_ref[0:G, :] = ksum_ref[...].astype(BF16)
    ka_ref[G:, :] = k_ref[...]
    gate_floor = NEG * BS * LOG2E

    QT = 2
    order = list(range(nb // QT - 1, -1, -1))

    def scores(i):
        t = order[i]
        nr = G + (t + 1) * QT * BS
        s_bufs[i % 2][0:nr, :] = lax.dot_general(ka_ref[0:nr, :], q_ref[pl.ds(t * QT * BS, QT * BS), :],
                                                 NT_DIMS, preferred_element_type=F32)

    scores(0)

    for j in range(nb):
        cols = pl.ds(j * BS, BS)
        vt_ref[0:dh, cols] = v_ref[cols, :].astype(F32).T.astype(BF16)
    vt_ref[dh:, :] = jnp.ones((vt_ref.shape[0] - dh, seq), BF16)

    key_idx = lax.broadcasted_iota(jnp.int32, (BS, BS), 0)
    qry_idx = lax.broadcasted_iota(jnp.int32, (BS, BS), 1)
    causal = key_idx <= qry_idx

    for i, t in enumerate(order):
        nk = (t + 1) * QT * BS
        if i + 1 < len(order):
            scores(i + 1)
        s_cur, p_cur = s_bufs[i % 2], p_bufs[i % 2]

        for hf in range(QT):
            n = t * QT + hf
            cs = pl.ds(hf * BS, BS)

            gate = [s_cur[j:j + 1, cs] + s_cur[nb + j:nb + j + 1, cs] for j in range(n)]

            sels = []
            for j in range(n):
                rank = jnp.where(gate[j] < gate_floor, float(nb - n), 0.0)
                for r in range(n):
                    if r != j:
                        ahead = (gate[r] >= gate[j]) if r < j else (gate[r] > gate[j])
                        rank = rank + ahead.astype(F32)
                sels.append(rank < MOBA_TOPK)

            s_own = jnp.where(causal, s_cur[pl.ds(G + n * BS, BS), cs], NEG)
            m = jnp.max(s_own, axis=0, keepdims=True)
            for j in range(n):
                m_blk = jnp.max(s_cur[pl.ds(G + j * BS, BS), cs], axis=0, keepdims=True)
                m = jnp.maximum(m, jnp.where(sels[j], m_blk, NEG))

            p_cur[pl.ds(n * BS, BS), cs] = jnp.exp2(s_own - m).astype(BF16)
            for j in range(n):
                p = jnp.exp2(s_cur[pl.ds(G + j * BS, BS), cs] - jnp.where(sels[j], m, -NEG))
                p_cur[pl.ds(j * BS, BS), cs] = p.astype(BF16)
            for j in range(n + 1, (t + 1) * QT):
                p_cur[pl.ds(j * BS, BS), cs] = jnp.zeros((BS, BS), BF16)

        acc = jnp.dot(vt_ref[:, 0:nk], p_cur[0:nk, :], preferred_element_type=F32)
        for hf in range(QT):
            cs = slice(hf * BS, (hf + 1) * BS)
            out = acc[0:dh, cs] * (1.0 / acc[dh:dh + 1, cs])
            o_ref[pl.ds((t * QT + hf) * BS, BS), col0:col0 + dh] = out.T.astype(o_ref.dtype)


def _moba(qkv, batch, seq):
    H, DH, HP = MOBA_HEADS, MOBA_DH, MOBA_HEADS_PER_STEP
    assert 2 * (seq // MOBA_BLOCK) == BF16_SUBLANES, "hi and lo key-sum rows must fill one bf16 tile"
    groups = H // HP
    per_head = [pltpu.VMEM((BF16_SUBLANES + seq, DH), BF16),
                pltpu.VMEM((BF16_SUBLANES, DH), F32),
                pltpu.VMEM((DH + BF16_SUBLANES, seq), BF16),
                pltpu.VMEM((BF16_SUBLANES + seq, 2 * MOBA_BLOCK), F32),
                pltpu.VMEM((BF16_SUBLANES + seq, 2 * MOBA_BLOCK), F32),
                pltpu.VMEM((seq, 2 * MOBA_BLOCK), BF16), pltpu.VMEM((seq, 2 * MOBA_BLOCK), BF16)]
    assert len(per_head) == MOBA_SCRATCH_PER_HEAD
    return pl.pallas_call(
        _moba_kernel,
        grid=(batch, groups),
        in_specs=[
            pl.BlockSpec((None, HP, seq, DH), lambda b, g: (b, g, 0, 0)),
            pl.BlockSpec((None, HP, seq, DH), lambda b, g: (b, groups + g, 0, 0)),
            pl.BlockSpec((None, HP, seq, DH), lambda b, g: (b, 2 * groups + g, 0, 0)),
        ],
        out_specs=pl.BlockSpec((None, seq, HP * DH), lambda b, g: (b, 0, g)),
        out_shape=jax.ShapeDtypeStruct((batch, seq, H * DH), BF16),
        scratch_shapes=per_head * HP,
        compiler_params=pltpu.CompilerParams(
            dimension_semantics=("arbitrary", "arbitrary"), vmem_limit_bytes=VMEM_LIMIT),
        name="moba",
    )(qkv, qkv, qkv)


def _outproj_mlp_kernel(a_ref, h_ref, wo_ref, gpost_ref, gpre_ref, gmpost_ref, wup_ref, wdn_ref,
                        o_ref, xn_ref, hid_ref, *, sub, ff_chunk):
    assert sum(sub) == a_ref.shape[0]
    subs = [pl.ds(sum(sub[:i]), sub[i]) for i in range(len(sub))]
    for rows in subs:
        mix = jnp.dot(a_ref[rows, :], wo_ref[...], preferred_element_type=F32)
        h1 = h_ref[rows, :] + _rms_norm(mix, gpost_ref[...])
        o_ref[rows, :] = h1
        xn_ref[rows, :] = _rms_norm(h1, gpre_ref[...]).astype(BF16)
    for rows in subs:
        for j in range(wup_ref.shape[1] // ff_chunk):
            cols = pl.ds(j * ff_chunk, ff_chunk)
            u = jnp.maximum(jnp.dot(xn_ref[rows, :], wup_ref[:, cols], preferred_element_type=F32), 0.0)
            hid_ref[rows, cols] = (u * u).astype(BF16)
    for rows in subs:
        dn = jnp.dot(hid_ref[rows, :], wdn_ref[...], preferred_element_type=F32)
        o_ref[rows, :] = o_ref[rows, :] + _rms_norm(dn, gmpost_ref[...])


def _outproj_mlp(a2d, h2d, w_out, g_post, g_mpre, g_mpost, w_up, w_down, *, tm=1024, sub=(256, 256, 256, 256), ff_chunk=1024):
    m_total, d = h2d.shape
    ka = a2d.shape[1]
    dff = w_up.shape[1]
    return pl.pallas_call(
        functools.partial(_outproj_mlp_kernel, sub=sub, ff_chunk=ff_chunk),
        grid=(m_total // tm,),
        in_specs=[
            pl.BlockSpec((tm, ka), lambda m: (m, 0)),
            pl.BlockSpec((tm, d), lambda m: (m, 0)),
            _resident((ka, d)),
            _resident((1, d)),
            _resident((1, d)),
            _resident((1, d)),
            _resident((d, dff)),
            _resident((dff, d)),
        ],
        out_specs=pl.BlockSpec((tm, d), lambda m: (m, 0)),
        out_shape=jax.ShapeDtypeStruct((m_total, d), F32),
        scratch_shapes=[pltpu.VMEM((tm, d), BF16), pltpu.VMEM((tm, dff), BF16)],
        compiler_params=pltpu.CompilerParams(
            dimension_semantics=("arbitrary",), vmem_limit_bytes=VMEM_LIMIT),
        name=f"outproj_mlp_k{ka}",
    )(a2d, h2d, w_out, g_post, g_mpre, g_mpost, w_up, w_down)


def _ret_tables(seq):
    freq = 1.0 / (RET_THETA ** jnp.linspace(0.0, 1.0, RET_DK // 2, dtype=F32))
    ang = jnp.arange(seq)[:, None].astype(F32) * freq[None, :]
    return jnp.cos(ang), jnp.sin(ang)


def _moba_tables(seq):
    half = ROPE_DIM // 2
    inv = ROPE_THETA ** (-jnp.arange(0, ROPE_DIM, 2, dtype=F32) / ROPE_DIM)
    ang = jnp.arange(seq)[:, None].astype(F32) * inv[None, :]
    cos, sin = jnp.cos(ang), jnp.sin(ang)
    gap = LANES // 2 - half
    c = jnp.concatenate([cos, jnp.ones((seq, gap), F32), cos, jnp.ones((seq, gap), F32)], axis=1)
    s = jnp.concatenate([-sin, jnp.zeros((seq, gap), F32), sin, jnp.zeros((seq, gap), F32)], axis=1)
    return c, s


def kernel(x, ln_mix_pre_0, ln_mix_post_0, w_in_0, w_out_0, ln_mlp_pre_0, ln_mlp_post_0, w_up_0, w_down_0,
           ln_mix_pre_1, ln_mix_post_1, w_in_1, w_out_1, ln_mlp_pre_1, ln_mlp_post_1, w_up_1, w_down_1):
    batch, seq, d = x.shape
    row = lambda g: g.reshape(1, d).astype(F32)

    log_gamma = jnp.log(1.0 - 2.0 ** (-5.0 - jnp.arange(RET_HEADS, dtype=F32)))

    h = x.reshape(batch * seq, d)

    mix = _proj_retention(h, row(ln_mix_pre_0), w_in_0.astype(BF16), _ret_tables(seq), log_gamma, seq=seq)
    h = _outproj_mlp(mix, h, w_out_0.astype(BF16), row(ln_mix_post_0),
                     row(ln_mlp_pre_0), row(ln_mlp_post_0), w_up_0.astype(BF16), w_down_0.astype(BF16))

    (qkv,) = _inproj(h, row(ln_mix_pre_1), w_in_1.astype(BF16), _moba_tables(seq), mode="moba", batch=batch,
                     seq=seq)
    mix = _moba(qkv, batch, seq)
    h = _outproj_mlp(mix.reshape(batch * seq, -1), h, w_out_1.astype(BF16), row(ln_mix_post_1),
                     row(ln_mlp_pre_1), row(ln_mlp_post_1), w_up_1.astype(BF16), w_down_1.astype(BF16))
    return h.reshape(batch, seq, d)
```

```python
import functools
import math

import jax
import jax.numpy as jnp
from jax import lax
from jax.experimental import pallas as pl
from jax.experimental.pallas import tpu as pltpu

D_MODEL = 1024
D_FF = 4 * D_MODEL
NORM_EPS = 1e-6
NEG = -1e30
LOG2E = math.log2(math.e)

RET_HEADS = 4
RET_DK = 256
RET_DV = 512
RET_THETA = 10000.0
RET_CHUNK = 256

MOBA_HEADS = 8
MOBA_DH = 128
MOBA_BLOCK = 256
MOBA_TOPK = 3
ROPE_THETA = 500000.0
ROPE_DIM = 32

LANES = 128
BF16_SUBLANES = 16
VMEM_LIMIT = 60 * 1024 * 1024

F32 = jnp.float32
BF16 = jnp.bfloat16
NT_DIMS = (((1,), (1,)), ((), ()))


def _rms_norm(x, g):
    return x * lax.rsqrt(jnp.mean(x * x, axis=-1, keepdims=True) + NORM_EPS) * g


def _resident(shape):
    return pl.BlockSpec(shape, lambda *_: (0,) * len(shape), pipeline_mode=pl.Buffered(1))


def _inproj_outputs(mode):
    if mode == "ret":
        return [(2 * RET_HEADS, RET_DK), (2 * RET_HEADS, RET_DV)]
    return [(3 * MOBA_HEADS, MOBA_DH)]


def _inproj_plan(mode):
    if mode == "ret":
        hq = RET_HEADS * RET_DK
        hv = RET_HEADS * RET_DV
        plan = [(1, hv + c * RET_DV, RET_DV, "swish", 1, RET_HEADS + c) for c in range(RET_HEADS)]
        plan += [(0, h * RET_DK, RET_DK, 1.0, 0, h) for h in range(RET_HEADS)]
        plan += [(0, hq + h * RET_DK, RET_DK, RET_DK ** -0.5, 0, RET_HEADS + h) for h in range(RET_HEADS)]
        plan += [(1, c * RET_DV, RET_DV, None, 1, c) for c in range(RET_HEADS)]
    else:
        hq = MOBA_HEADS * MOBA_DH
        per = 4
        cw = per * MOBA_DH
        plan = [(0, c * cw, cw, MOBA_DH ** -0.5 * LOG2E, 0, c * per) for c in range(hq // cw)]
        plan += [(0, hq + c * cw, cw, 1.0, 0, MOBA_HEADS + c * per) for c in range(hq // cw)]
        plan += [(1, c * cw, cw, None, 0, 2 * MOBA_HEADS + c * per) for c in range(hq // cw)]
    return plan


PERM_W = 256
NORM_PIECE = 256


def _qk_column_permutation(mode):
    new = lax.broadcasted_iota(jnp.int32, (PERM_W, PERM_W), 1)
    old = lax.broadcasted_iota(jnp.int32, (PERM_W, PERM_W), 0)
    if mode == "ret":
        half = RET_DK // 2
        src = jnp.where(new < half, 2 * new, 2 * (new - half) + 1)
    else:
        j = new % MOBA_DH
        h16, mid = ROPE_DIM // 2, MOBA_DH // 2
        src_j = jnp.where(j < h16, j, jnp.where(j < mid, j + h16, jnp.where(j < mid + h16, j - (mid - h16), j)))
        src = (new - j) + src_j
    return (old == src).astype(BF16)


def _inproj_kernel(x_ref, g_ref, w_ref, cos_ref, sin_ref, *rest, mode, n_cast):
    cast_in, rest = rest[:n_cast], rest[n_cast:]
    n_out = len(rest) - n_cast - 2
    out_refs, cast_out, (xs_ref, wqk_ref) = rest[:n_out], rest[n_out:n_out + n_cast], rest[n_out + n_cast:]
    n_qk = wqk_ref.shape[1]
    for src, dst in zip(cast_in, cast_out):
        dst[...] = src[...].astype(dst.dtype)

    @pl.when(pl.program_id(0) == 0)
    def _():
        perm = _qk_column_permutation(mode)
        for c0 in range(0, n_qk, PERM_W):
            cols = slice(c0, c0 + PERM_W)
            wqk_ref[:, cols] = jnp.dot(w_ref[:, cols], perm, preferred_element_type=F32).astype(BF16)

    tm = x_ref.shape[0]
    pieces = [pl.ds(r, NORM_PIECE) for r in range(0, tm, NORM_PIECE)]
    for rows in pieces:
        xs_ref[rows, :] = _rms_norm(x_ref[rows, :], g_ref[...]).astype(BF16)
    scaled = {}
    for idx, (wi, start, width, scale, oi, slab0) in enumerate(_inproj_plan(mode)):
        o_ref = out_refs[oi]
        sw = o_ref.shape[-1]
        w_cols = wqk_ref[:, start:start + width] if wi == 0 else w_ref[:, n_qk + start:n_qk + start + width]
        for rows in (pieces if idx == 0 else [pl.ds(0, tm)]):
            acc = jnp.dot(xs_ref[rows, :], w_cols, preferred_element_type=F32)
            if scale is None or scale == "swish":
                if scale == "swish":
                    half = 0.5 * acc
                    acc = half + half * jnp.tanh(half)
                for i in range(width // sw):
                    o_ref[0, slab0 + i, rows, :] = acc[:, i * sw:(i + 1) * sw].astype(o_ref.dtype)
                continue
            if (scale, rows.start, rows.size) not in scaled:
                scaled[(scale, rows.start, rows.size)] = (cos_ref[rows, :] * scale, sin_ref[rows, :] * scale)
            c, s = scaled[(scale, rows.start, rows.size)]
            if mode == "ret":
                x1, x2 = acc[:, :LANES], acc[:, LANES:]
                o_ref[0, slab0, rows, 0:LANES] = (x1 * c - x2 * s).astype(o_ref.dtype)
                o_ref[0, slab0, rows, LANES:width] = (x1 * s + x2 * c).astype(o_ref.dtype)
            else:
                for i in range(width // sw):
                    xh = acc[:, i * sw:(i + 1) * sw]
                    y = xh * c + pltpu.roll(xh, LANES // 2, 1) * s
                    o_ref[0, slab0 + i, rows, :] = y.astype(o_ref.dtype)


def _inproj(x2d, gain, w, tabs, *, mode, batch, seq, cast=(), tm=1024):
    m_total, d = x2d.shape
    n_qk = 2 * (RET_HEADS * RET_DK if mode == "ret" else MOBA_HEADS * MOBA_DH)
    tiles_per_seq = seq // tm
    steps = m_total // tm
    tab_spec = pl.BlockSpec((tm, LANES), lambda m: (m % tiles_per_seq, 0))
    outs = _inproj_outputs(mode)
    for a in cast:
        assert a.shape[0] % (steps * BF16_SUBLANES) == 0, a.shape
    cast_specs = [pl.BlockSpec((a.shape[0] // steps, a.shape[1]), lambda m: (m, 0)) for a in cast]
    res = pl.pallas_call(
        functools.partial(_inproj_kernel, mode=mode, n_cast=len(cast)),
        grid=(steps,),
        in_specs=[pl.BlockSpec((tm, d), lambda m: (m, 0)), _resident((1, d)), _resident(w.shape),
                  tab_spec, tab_spec, *cast_specs],
        out_specs=[pl.BlockSpec((1, ns, tm, sw), lambda m: (m // tiles_per_seq, 0, m % tiles_per_seq, 0))
                   for ns, sw in outs] + cast_specs,
        out_shape=[jax.ShapeDtypeStruct((batch, ns, seq, sw), BF16) for ns, sw in outs]
        + [jax.ShapeDtypeStruct(a.shape, BF16) for a in cast],
        scratch_shapes=[pltpu.VMEM((tm, d), BF16), pltpu.VMEM((d, n_qk), BF16)],
        compiler_params=pltpu.CompilerParams(
            dimension_semantics=("arbitrary",), vmem_limit_bytes=VMEM_LIMIT),
        name=f"inproj_{mode}",
    )(x2d, gain, w, *tabs, *cast)
    return res[:len(outs)], res[len(outs):]


RET_HEADS_PER_STEP = 2


def _retention_decays(lg):
    L = RET_CHUNK
    ii = lax.broadcasted_iota(jnp.int32, (L, L), 0)
    jj = lax.broadcasted_iota(jnp.int32, (L, L), 1)
    diff = (ii - jj).astype(F32)
    d_intra = jnp.where(diff >= 0, jnp.exp(lg * jnp.maximum(diff, 0.0)), 0.0)
    col = lax.broadcasted_iota(jnp.int32, (L, 1), 0).astype(F32)
    q_decay = jnp.exp(lg * (col + 1.0))
    row = lax.broadcasted_iota(jnp.int32, (1, L), 1).astype(F32)
    k_decay = jnp.exp(lg * (L - 1.0 - row))
    chunk_decay = jnp.exp(jnp.full((1, 1), lg * L, F32))
    return d_intra, q_decay, k_decay, chunk_decay


def _retention_kernel(lg_ref, q_ref, k_ref, v_ref, g_ref, o_ref, state_ref):
    L = RET_CHUNK
    hp, seq, _ = q_ref.shape
    dv = v_ref.shape[2]
    head0 = pl.program_id(1) * hp
    decays = [_retention_decays(lg_ref[head0 + i]) for i in range(hp)]

    state_ref[...] = jnp.zeros_like(state_ref)
    for c in range(seq // L):
        rows = pl.ds(c * L, L)
        for i in range(hp):
            d_intra, q_decay, k_decay, chunk_decay = decays[i]
            qc = q_ref[i, rows, :]
            kc = k_ref[i, rows, :]
            vc = v_ref[i, rows, :]
            scores = lax.dot_general(qc, kc, NT_DIMS, preferred_element_type=F32) * d_intra
            o = jnp.dot(scores.astype(BF16), vc, preferred_element_type=F32)
            state = state_ref[i]
            o = o + jnp.dot(qc, state.astype(BF16), preferred_element_type=F32) * q_decay
            kd_t = (kc.astype(F32).T * k_decay).astype(BF16)
            state_ref[i] = state * chunk_decay + jnp.dot(kd_t, vc, preferred_element_type=F32)
            o = o * lax.rsqrt(jnp.mean(o * o, axis=-1, keepdims=True) + NORM_EPS)
            o_ref[rows, i * dv:(i + 1) * dv] = (o * g_ref[i, rows, :].astype(F32)).astype(o_ref.dtype)


def _retention(qk, vg, log_gamma, batch, seq):
    H, DK, DV, HP = RET_HEADS, RET_DK, RET_DV, RET_HEADS_PER_STEP
    groups = H // HP
    grid_spec = pltpu.PrefetchScalarGridSpec(
        num_scalar_prefetch=1,
        grid=(batch, groups),
        in_specs=[
            pl.BlockSpec((None, HP, seq, DK), lambda b, g, lg: (b, g, 0, 0)),
            pl.BlockSpec((None, HP, seq, DK), lambda b, g, lg: (b, groups + g, 0, 0)),
            pl.BlockSpec((None, HP, seq, DV), lambda b, g, lg: (b, g, 0, 0)),
            pl.BlockSpec((None, HP, seq, DV), lambda b, g, lg: (b, groups + g, 0, 0)),
        ],
        out_specs=pl.BlockSpec((None, seq, HP * DV), lambda b, g, lg: (b, 0, g)),
        scratch_shapes=[pltpu.VMEM((HP, DK, DV), F32)],
    )
    return pl.pallas_call(
        _retention_kernel,
        grid_spec=grid_spec,
        out_shape=jax.ShapeDtypeStruct((batch, seq, H * DV), BF16),
        compiler_params=pltpu.CompilerParams(
            dimension_semantics=("arbitrary", "arbitrary"), vmem_limit_bytes=VMEM_LIMIT),
        name="retention",
    )(log_gamma, qk, qk, vg, vg)


MOBA_HEADS_PER_STEP = 2
MOBA_SCRATCH_PER_HEAD = 7


def _moba_kernel(q_ref, k_ref, v_ref, o_ref, *scratch):
    dh = q_ref.shape[2]
    for i in range(q_ref.shape[0]):
        bufs = scratch[i * MOBA_SCRATCH_PER_HEAD:(i + 1) * MOBA_SCRATCH_PER_HEAD]
        _moba_head(q_ref.at[i], k_ref.at[i], v_ref.at[i], o_ref, i * dh, *bufs)


def _moba_head(q_ref, k_ref, v_ref, o_ref, col0, ka_ref, ksum_ref, vt_ref, s0_ref, s1_ref, p0_ref, p1_ref):
    s_bufs, p_bufs = (s0_ref, s1_ref), (p0_ref, p1_ref)
    BS = MOBA_BLOCK
    seq, dh = q_ref.shape
    nb = seq // BS
    G = ka_ref.shape[0] - seq

    for j in range(nb):
        ksum = jnp.sum(k_ref[pl.ds(j * BS, BS), :].astype(F32), axis=0, keepdims=True)
        hi = ksum.astype(BF16).astype(F32)
        ksum_ref[j:j + 1, :] = hi
        ksum_ref[nb + j:nb + j + 1, :] = ksum - hi
    ka_ref[0:G, :] = ksum_ref[...].astype(BF16)
    ka_ref[G:, :] = k_ref[...]
    gate_floor = NEG * BS * LOG2E

    QT = 2
    order = list(range(nb // QT - 1, -1, -1))

    def scores(i):
        t = order[i]
        nr = G + (t + 1) * QT * BS
        s_bufs[i % 2][0:nr, :] = lax.dot_general(ka_ref[0:nr, :], q_ref[pl.ds(t * QT * BS, QT * BS), :],
                                                 NT_DIMS, preferred_element_type=F32)

    scores(0)

    for j in range(nb):
        cols = pl.ds(j * BS, BS)
        vt_ref[0:dh, cols] = v_ref[cols, :].astype(F32).T.astype(BF16)
    vt_ref[dh:, :] = jnp.ones((vt_ref.shape[0] - dh, seq), BF16)

    key_idx = lax.broadcasted_iota(jnp.int32, (BS, BS), 0)
    qry_idx = lax.broadcasted_iota(jnp.int32, (BS, BS), 1)
    causal = key_idx <= qry_idx

    for i, t in enumerate(order):
        nk = (t + 1) * QT * BS
        if i + 1 < len(order):
            scores(i + 1)
        s_cur, p_cur = s_bufs[i % 2], p_bufs[i % 2]

        for hf in range(QT):
            n = t * QT + hf
            cs = pl.ds(hf * BS, BS)

            gate = [s_cur[j:j + 1, cs] + s_cur[nb + j:nb + j + 1, cs] for j in range(n)]

            sels = []
            for j in range(n):
                rank = jnp.where(gate[j] < gate_floor, float(nb - n), 0.0)
                for r in range(n):
                    if r != j:
                        ahead = (gate[r] >= gate[j]) if r < j else (gate[r] > gate[j])
                        rank = rank + ahead.astype(F32)
                sels.append(rank < MOBA_TOPK)

            s_own = jnp.where(causal, s_cur[pl.ds(G + n * BS, BS), cs], NEG)
            m = jnp.max(s_own, axis=0, keepdims=True)
            for j in range(n):
                m_blk = jnp.max(s_cur[pl.ds(G + j * BS, BS), cs], axis=0, keepdims=True)
                m = jnp.maximum(m, jnp.where(sels[j], m_blk, NEG))

            p_cur[pl.ds(n * BS, BS), cs] = jnp.exp2(s_own - m).astype(BF16)
            for j in range(n):
                p = jnp.exp2(s_cur[pl.ds(G + j * BS, BS), cs] - jnp.where(sels[j], m, -NEG))
                p_cur[pl.ds(j * BS, BS), cs] = p.astype(BF16)
            for j in range(n + 1, (t + 1) * QT):
                p_cur[pl.ds(j * BS, BS), cs] = jnp.zeros((BS, BS), BF16)

        acc = jnp.dot(vt_ref[:, 0:nk], p_cur[0:nk, :], preferred_element_type=F32)
        for hf in range(QT):
            cs = slice(hf * BS, (hf + 1) * BS)
            out = acc[0:dh, cs] * (1.0 / acc[dh:dh + 1, cs])
            o_ref[pl.ds((t * QT + hf) * BS, BS), col0:col0 + dh] = out.T.astype(o_ref.dtype)


def _moba(qkv, batch, seq):
    H, DH, HP = MOBA_HEADS, MOBA_DH, MOBA_HEADS_PER_STEP
    assert 2 * (seq // MOBA_BLOCK) == BF16_SUBLANES, "hi and lo key-sum rows must fill one bf16 tile"
    groups = H // HP
    per_head = [pltpu.VMEM((BF16_SUBLANES + seq, DH), BF16),
                pltpu.VMEM((BF16_SUBLANES, DH), F32),
                pltpu.VMEM((DH + BF16_SUBLANES, seq), BF16),
                pltpu.VMEM((BF16_SUBLANES + seq, 2 * MOBA_BLOCK), F32),
                pltpu.VMEM((BF16_SUBLANES + seq, 2 * MOBA_BLOCK), F32),
                pltpu.VMEM((seq, 2 * MOBA_BLOCK), BF16), pltpu.VMEM((seq, 2 * MOBA_BLOCK), BF16)]
    assert len(per_head) == MOBA_SCRATCH_PER_HEAD
    return pl.pallas_call(
        _moba_kernel,
        grid=(batch, groups),
        in_specs=[
            pl.BlockSpec((None, HP, seq, DH), lambda b, g: (b, g, 0, 0)),
            pl.BlockSpec((None, HP, seq, DH), lambda b, g: (b, groups + g, 0, 0)),
            pl.BlockSpec((None, HP, seq, DH), lambda b, g: (b, 2 * groups + g, 0, 0)),
        ],
        out_specs=pl.BlockSpec((None, seq, HP * DH), lambda b, g: (b, 0, g)),
        out_shape=jax.ShapeDtypeStruct((batch, seq, H * DH), BF16),
        scratch_shapes=per_head * HP,
        compiler_params=pltpu.CompilerParams(
            dimension_semantics=("arbitrary", "arbitrary"), vmem_limit_bytes=VMEM_LIMIT),
        name="moba",
    )(qkv, qkv, qkv)


def _outproj_mlp_kernel(a_ref, h_ref, wo_ref, gpost_ref, gpre_ref, gmpost_ref, wup_ref, wdn_ref,
                        o_ref, xn_ref, hid_ref, *, sub, ff_chunk):
    assert sum(sub) == a_ref.shape[0]
    subs = [pl.ds(sum(sub[:i]), sub[i]) for i in range(len(sub))]
    for rows in subs:
        mix = jnp.dot(a_ref[rows, :], wo_ref[...], preferred_element_type=F32)
        h1 = h_ref[rows, :] + _rms_norm(mix, gpost_ref[...])
        o_ref[rows, :] = h1
        xn_ref[rows, :] = _rms_norm(h1, gpre_ref[...]).astype(BF16)
    for rows in subs:
        for j in range(wup_ref.shape[1] // ff_chunk):
            cols = pl.ds(j * ff_chunk, ff_chunk)
            u = jnp.maximum(jnp.dot(xn_ref[rows, :], wup_ref[:, cols], preferred_element_type=F32), 0.0)
            hid_ref[rows, cols] = (u * u).astype(BF16)
    for rows in subs:
        dn = jnp.dot(hid_ref[rows, :], wdn_ref[...], preferred_element_type=F32)
        o_ref[rows, :] = o_ref[rows, :] + _rms_norm(dn, gmpost_ref[...])


def _outproj_mlp(a2d, h2d, w_out, g_post, g_mpre, g_mpost, w_up, w_down, *, tm=1024, sub=(256, 256, 256, 256), ff_chunk=1024):
    m_total, d = h2d.shape
    ka = a2d.shape[1]
    dff = w_up.shape[1]
    return pl.pallas_call(
        functools.partial(_outproj_mlp_kernel, sub=sub, ff_chunk=ff_chunk),
        grid=(m_total // tm,),
        in_specs=[
            pl.BlockSpec((tm, ka), lambda m: (m, 0)),
            pl.BlockSpec((tm, d), lambda m: (m, 0)),
            _resident((ka, d)),
            _resident((1, d)),
            _resident((1, d)),
            _resident((1, d)),
            _resident((d, dff)),
            _resident((dff, d)),
        ],
        out_specs=pl.BlockSpec((tm, d), lambda m: (m, 0)),
        out_shape=jax.ShapeDtypeStruct((m_total, d), F32),
        scratch_shapes=[pltpu.VMEM((tm, d), BF16), pltpu.VMEM((tm, dff), BF16)],
        compiler_params=pltpu.CompilerParams(
            dimension_semantics=("arbitrary",), vmem_limit_bytes=VMEM_LIMIT),
        name=f"outproj_mlp_k{ka}",
    )(a2d, h2d, w_out, g_post, g_mpre, g_mpost, w_up, w_down)


def _ret_tables(seq):
    freq = 1.0 / (RET_THETA ** jnp.linspace(0.0, 1.0, RET_DK // 2, dtype=F32))
    ang = jnp.arange(seq)[:, None].astype(F32) * freq[None, :]
    return jnp.cos(ang), jnp.sin(ang)


def _moba_tables(seq):
    half = ROPE_DIM // 2
    inv = ROPE_THETA ** (-jnp.arange(0, ROPE_DIM, 2, dtype=F32) / ROPE_DIM)
    ang = jnp.arange(seq)[:, None].astype(F32) * inv[None, :]
    cos, sin = jnp.cos(ang), jnp.sin(ang)
    gap = LANES // 2 - half
    c = jnp.concatenate([cos, jnp.ones((seq, gap), F32), cos, jnp.ones((seq, gap), F32)], axis=1)
    s = jnp.concatenate([-sin, jnp.zeros((seq, gap), F32), sin, jnp.zeros((seq, gap), F32)], axis=1)
    return c, s


def kernel(x, ln_mix_pre_0, ln_mix_post_0, w_in_0, w_out_0, ln_mlp_pre_0, ln_mlp_post_0, w_up_0, w_down_0,
           ln_mix_pre_1, ln_mix_post_1, w_in_1, w_out_1, ln_mlp_pre_1, ln_mlp_post_1, w_up_1, w_down_1):
    batch, seq, d = x.shape
    row = lambda g: g.reshape(1, d).astype(F32)

    log_gamma = jnp.log(1.0 - 2.0 ** (-5.0 - jnp.arange(RET_HEADS, dtype=F32)))

    h = x.reshape(batch * seq, d)

    (qk, vg), (w_out_0b, w_up_0b, w_down_0b, w_in_1b) = _inproj(
        h, row(ln_mix_pre_0), w_in_0.astype(BF16), _ret_tables(seq), mode="ret", batch=batch, seq=seq,
        cast=(w_out_0, w_up_0, w_down_0, w_in_1))
    mix = _retention(qk, vg, log_gamma, batch, seq)
    h = _outproj_mlp(mix.reshape(batch * seq, -1), h, w_out_0b, row(ln_mix_post_0),
                     row(ln_mlp_pre_0), row(ln_mlp_post_0), w_up_0b, w_down_0b)

    (qkv,), (w_out_1b, w_up_1b, w_down_1b) = _inproj(
        h, row(ln_mix_pre_1), w_in_1b, _moba_tables(seq), mode="moba", batch=batch, seq=seq,
        cast=(w_out_1, w_up_1, w_down_1))
    mix = _moba(qkv, batch, seq)
    h = _outproj_mlp(mix.reshape(batch * seq, -1), h, w_out_1b, row(ln_mix_post_1),
                     row(ln_mlp_pre_1), row(ln_mlp_post_1), w_up_1b, w_down_1b)
    return h.reshape(batch, seq, d)
```

```python
import functools
import math

import jax
import jax.numpy as jnp
from jax import lax
from jax.experimental import pallas as pl
from jax.experimental.pallas import tpu as pltpu

D_MODEL = 1024
D_FF = 4 * D_MODEL
NORM_EPS = 1e-6
NEG = -1e30
LOG2E = math.log2(math.e)

RET_HEADS = 4
RET_DK = 256
RET_DV = 512
RET_THETA = 10000.0
RET_CHUNK = 256

MOBA_HEADS = 8
MOBA_DH = 128
MOBA_BLOCK = 256
MOBA_TOPK = 3
ROPE_THETA = 500000.0
ROPE_DIM = 32

LANES = 128
BF16_SUBLANES = 16
VMEM_LIMIT = 60 * 1024 * 1024

F32 = jnp.float32
BF16 = jnp.bfloat16
NT_DIMS = (((1,), (1,)), ((), ()))


def _rms_norm(x, g):
    return x * lax.rsqrt(jnp.mean(x * x, axis=-1, keepdims=True) + NORM_EPS) * g


def _resident(shape):
    return pl.BlockSpec(shape, lambda *_: (0,) * len(shape), pipeline_mode=pl.Buffered(1))


def _inproj_outputs(mode):
    if mode == "ret":
        return [(2 * RET_HEADS, RET_DK), (2 * RET_HEADS, RET_DV)]
    return [(3 * MOBA_HEADS, MOBA_DH)]


def _inproj_plan(mode):
    if mode == "ret":
        hq = RET_HEADS * RET_DK
        hv = RET_HEADS * RET_DV
        plan = [(1, hv + c * RET_DV, RET_DV, "swish", 1, RET_HEADS + c) for c in range(RET_HEADS)]
        plan += [(0, h * RET_DK, RET_DK, 1.0, 0, h) for h in range(RET_HEADS)]
        plan += [(0, hq + h * RET_DK, RET_DK, RET_DK ** -0.5, 0, RET_HEADS + h) for h in range(RET_HEADS)]
        plan += [(1, c * RET_DV, RET_DV, None, 1, c) for c in range(RET_HEADS)]
    else:
        hq = MOBA_HEADS * MOBA_DH
        per = 4
        cw = per * MOBA_DH
        plan = [(0, c * cw, cw, MOBA_DH ** -0.5 * LOG2E, 0, c * per) for c in range(hq // cw)]
        plan += [(0, hq + c * cw, cw, 1.0, 0, MOBA_HEADS + c * per) for c in range(hq // cw)]
        plan += [(1, c * cw, cw, None, 0, 2 * MOBA_HEADS + c * per) for c in range(hq // cw)]
    return plan


PERM_W = 256
NORM_PIECE = 256


def _qk_column_permutation(mode):
    new = lax.broadcasted_iota(jnp.int32, (PERM_W, PERM_W), 1)
    old = lax.broadcasted_iota(jnp.int32, (PERM_W, PERM_W), 0)
    if mode == "ret":
        half = RET_DK // 2
        src = jnp.where(new < half, 2 * new, 2 * (new - half) + 1)
    else:
        j = new % MOBA_DH
        h16, mid = ROPE_DIM // 2, MOBA_DH // 2
        src_j = jnp.where(j < h16, j, jnp.where(j < mid, j + h16, jnp.where(j < mid + h16, j - (mid - h16), j)))
        src = (new - j) + src_j
    return (old == src).astype(BF16)


def _inproj_kernel(x_ref, g_ref, w_ref, cos_ref, sin_ref, *rest, mode, n_cast):
    cast_in, rest = rest[:n_cast], rest[n_cast:]
    n_out = len(rest) - n_cast - 2
    out_refs, cast_out, (xs_ref, wqk_ref) = rest[:n_out], rest[n_out:n_out + n_cast], rest[n_out + n_cast:]
    n_qk = wqk_ref.shape[1]
    for src, dst in zip(cast_in, cast_out):
        dst[...] = src[...].astype(dst.dtype)

    @pl.when(pl.program_id(0) == 0)
    def _():
        perm = _qk_column_permutation(mode)
        for c0 in range(0, n_qk, PERM_W):
            cols = slice(c0, c0 + PERM_W)
            wqk_ref[:, cols] = jnp.dot(w_ref[:, cols], perm, preferred_element_type=F32).astype(BF16)

    tm = x_ref.shape[0]
    pieces = [pl.ds(r, NORM_PIECE) for r in range(0, tm, NORM_PIECE)]
    for rows in pieces:
        xs_ref[rows, :] = _rms_norm(x_ref[rows, :], g_ref[...]).astype(BF16)
    scaled = {}
    for idx, (wi, start, width, scale, oi, slab0) in enumerate(_inproj_plan(mode)):
        o_ref = out_refs[oi]
        sw = o_ref.shape[-1]
        w_cols = wqk_ref[:, start:start + width] if wi == 0 else w_ref[:, n_qk + start:n_qk + start + width]
        for rows in (pieces if idx == 0 else [pl.ds(0, tm)]):
            acc = jnp.dot(xs_ref[rows, :], w_cols, preferred_element_type=F32)
            if scale is None or scale == "swish":
                if scale == "swish":
                    half = 0.5 * acc
                    acc = half + half * jnp.tanh(half)
                for i in range(width // sw):
                    o_ref[0, slab0 + i, rows, :] = acc[:, i * sw:(i + 1) * sw].astype(o_ref.dtype)
                continue
            if (scale, rows.start, rows.size) not in scaled:
                scaled[(scale, rows.start, rows.size)] = (cos_ref[rows, :] * scale, sin_ref[rows, :] * scale)
            c, s = scaled[(scale, rows.start, rows.size)]
            if mode == "ret":
                x1, x2 = acc[:, :LANES], acc[:, LANES:]
                o_ref[0, slab0, rows, 0:LANES] = (x1 * c - x2 * s).astype(o_ref.dtype)
                o_ref[0, slab0, rows, LANES:width] = (x1 * s + x2 * c).astype(o_ref.dtype)
            else:
                for i in range(width // sw):
                    xh = acc[:, i * sw:(i + 1) * sw]
                    y = xh * c + pltpu.roll(xh, LANES // 2, 1) * s
                    o_ref[0, slab0 + i, rows, :] = y.astype(o_ref.dtype)


def _inproj(x2d, gain, w, tabs, *, mode, batch, seq, cast=(), tm=1024):
    m_total, d = x2d.shape
    n_qk = 2 * (RET_HEADS * RET_DK if mode == "ret" else MOBA_HEADS * MOBA_DH)
    tiles_per_seq = seq // tm
    steps = m_total // tm
    tab_spec = pl.BlockSpec((tm, LANES), lambda m: (m % tiles_per_seq, 0))
    outs = _inproj_outputs(mode)
    for a in cast:
        assert a.shape[0] % (steps * BF16_SUBLANES) == 0, a.shape
    cast_specs = [pl.BlockSpec((a.shape[0] // steps, a.shape[1]), lambda m: (m, 0)) for a in cast]
    res = pl.pallas_call(
        functools.partial(_inproj_kernel, mode=mode, n_cast=len(cast)),
        grid=(steps,),
        in_specs=[pl.BlockSpec((tm, d), lambda m: (m, 0)), _resident((1, d)), _resident(w.shape),
                  tab_spec, tab_spec, *cast_specs],
        out_specs=[pl.BlockSpec((1, ns, tm, sw), lambda m: (m // tiles_per_seq, 0, m % tiles_per_seq, 0))
                   for ns, sw in outs] + cast_specs,
        out_shape=[jax.ShapeDtypeStruct((batch, ns, seq, sw), BF16) for ns, sw in outs]
        + [jax.ShapeDtypeStruct(a.shape, BF16) for a in cast],
        scratch_shapes=[pltpu.VMEM((tm, d), BF16), pltpu.VMEM((d, n_qk), BF16)],
        compiler_params=pltpu.CompilerParams(
            dimension_semantics=("arbitrary",), vmem_limit_bytes=VMEM_LIMIT),
        name=f"inproj_{mode}",
    )(x2d, gain, w, *tabs, *cast)
    return res[:len(outs)], res[len(outs):]


RET_HEADS_PER_STEP = 2


def _retention_decays(lg):
    L = RET_CHUNK
    ii = lax.broadcasted_iota(jnp.int32, (L, L), 0)
    jj = lax.broadcasted_iota(jnp.int32, (L, L), 1)
    diff = (ii - jj).astype(F32)
    d_intra = jnp.where(diff >= 0, jnp.exp(lg * jnp.maximum(diff, 0.0)), 0.0)
    col = lax.broadcasted_iota(jnp.int32, (L, 1), 0).astype(F32)
    q_decay = jnp.exp(lg * (col + 1.0))
    row = lax.broadcasted_iota(jnp.int32, (1, L), 1).astype(F32)
    k_decay = jnp.exp(lg * (L - 1.0 - row))
    chunk_decay = jnp.exp(jnp.full((1, 1), lg * L, F32))
    return d_intra, q_decay, k_decay, chunk_decay


def _retention_kernel(lg_ref, q_ref, k_ref, v_ref, g_ref, o_ref, state_ref):
    L = RET_CHUNK
    hp, seq, _ = q_ref.shape
    dv = v_ref.shape[2]
    head0 = pl.program_id(1) * hp
    decays = [_retention_decays(lg_ref[head0 + i]) for i in range(hp)]

    state_ref[...] = jnp.zeros_like(state_ref)
    for c in range(seq // L):
        rows = pl.ds(c * L, L)
        for i in range(hp):
            d_intra, q_decay, k_decay, chunk_decay = decays[i]
            qc = q_ref[i, rows, :]
            kc = k_ref[i, rows, :]
            vc = v_ref[i, rows, :]
            scores = lax.dot_general(qc, kc, NT_DIMS, preferred_element_type=F32) * d_intra
            o = jnp.dot(scores.astype(BF16), vc, preferred_element_type=F32)
            state = state_ref[i]
            o = o + jnp.dot(qc, state.astype(BF16), preferred_element_type=F32) * q_decay
            kd_t = (kc.astype(F32).T * k_decay).astype(BF16)
            state_ref[i] = state * chunk_decay + jnp.dot(kd_t, vc, preferred_element_type=F32)
            o = o * lax.rsqrt(jnp.mean(o * o, axis=-1, keepdims=True) + NORM_EPS)
            o_ref[rows, i * dv:(i + 1) * dv] = (o * g_ref[i, rows, :].astype(F32)).astype(o_ref.dtype)


def _retention(qk, vg, log_gamma, batch, seq):
    H, DK, DV, HP = RET_HEADS, RET_DK, RET_DV, RET_HEADS_PER_STEP
    groups = H // HP
    grid_spec = pltpu.PrefetchScalarGridSpec(
        num_scalar_prefetch=1,
        grid=(batch, groups),
        in_specs=[
            pl.BlockSpec((None, HP, seq, DK), lambda b, g, lg: (b, g, 0, 0)),
            pl.BlockSpec((None, HP, seq, DK), lambda b, g, lg: (b, groups + g, 0, 0)),
            pl.BlockSpec((None, HP, seq, DV), lambda b, g, lg: (b, g, 0, 0)),
            pl.BlockSpec((None, HP, seq, DV), lambda b, g, lg: (b, groups + g, 0, 0)),
        ],
        out_specs=pl.BlockSpec((None, seq, HP * DV), lambda b, g, lg: (b, 0, g)),
        scratch_shapes=[pltpu.VMEM((HP, DK, DV), F32)],
    )
    return pl.pallas_call(
        _retention_kernel,
        grid_spec=grid_spec,
        out_shape=jax.ShapeDtypeStruct((batch, seq, H * DV), BF16),
        compiler_params=pltpu.CompilerParams(
            dimension_semantics=("arbitrary", "arbitrary"), vmem_limit_bytes=VMEM_LIMIT),
        name="retention",
    )(log_gamma, qk, qk, vg, vg)


MOBA_HEADS_PER_STEP = 2
MOBA_SCORE_BUFFERS = 2
MOBA_SCRATCH_PER_HEAD = 3 + 2 * MOBA_SCORE_BUFFERS


def _moba_kernel(q_ref, k_ref, v_ref, o_ref, *scratch):
    dh = q_ref.shape[2]
    heads = []
    for i in range(q_ref.shape[0]):
        bufs = scratch[i * MOBA_SCRATCH_PER_HEAD:(i + 1) * MOBA_SCRATCH_PER_HEAD]
        heads.append(_moba_head(q_ref.at[i], k_ref.at[i], v_ref.at[i], o_ref, i * dh, *bufs))
    done = object()
    while heads:
        heads = [g for g in heads if next(g, done) is not done]


def _moba_head(q_ref, k_ref, v_ref, o_ref, col0, ka_ref, ksum_ref, vt_ref, *sp_refs):
    nbuf = len(sp_refs) // 2
    s_bufs, p_bufs = sp_refs[:nbuf], sp_refs[nbuf:]
    BS = MOBA_BLOCK
    seq, dh = q_ref.shape
    nb = seq // BS
    G = ka_ref.shape[0] - seq

    for j in range(nb):
        ksum = jnp.sum(k_ref[pl.ds(j * BS, BS), :].astype(F32), axis=0, keepdims=True)
        hi = ksum.astype(BF16).astype(F32)
        ksum_ref[j:j + 1, :] = hi
        ksum_ref[nb + j:nb + j + 1, :] = ksum - hi
    ka_ref[0:G, :] = ksum_ref[...].astype(BF16)
    ka_ref[G:, :] = k_ref[...]
    gate_floor = NEG * BS * LOG2E

    QT = 2
    order = list(range(nb // QT - 1, -1, -1))

    def scores(i):
        t = order[i]
        nr = G + (t + 1) * QT * BS
        s_bufs[i % nbuf][0:nr, :] = lax.dot_general(ka_ref[0:nr, :], q_ref[pl.ds(t * QT * BS, QT * BS), :],
                                                 NT_DIMS, preferred_element_type=F32)

    scores(0)

    for j in range(nb):
        cols = pl.ds(j * BS, BS)
        vt_ref[0:dh, cols] = v_ref[cols, :].astype(F32).T.astype(BF16)
    vt_ref[dh:, :] = jnp.ones((vt_ref.shape[0] - dh, seq), BF16)

    key_idx = lax.broadcasted_iota(jnp.int32, (BS, BS), 0)
    qry_idx = lax.broadcasted_iota(jnp.int32, (BS, BS), 1)
    causal = key_idx <= qry_idx
    yield

    for i, t in enumerate(order):
        nk = (t + 1) * QT * BS
        if nbuf > 1 and i + 1 < len(order):
            scores(i + 1)
        if nbuf == 1 and i > 0:
            scores(i)
        s_cur, p_cur = s_bufs[i % nbuf], p_bufs[i % nbuf]

        for hf in range(QT):
            n = t * QT + hf
            cs = pl.ds(hf * BS, BS)

            gate = [s_cur[j:j + 1, cs] + s_cur[nb + j:nb + j + 1, cs] for j in range(n)]

            sels = []
            for j in range(n):
                rank = jnp.where(gate[j] < gate_floor, float(nb - n), 0.0)
                for r in range(n):
                    if r != j:
                        ahead = (gate[r] >= gate[j]) if r < j else (gate[r] > gate[j])
                        rank = rank + ahead.astype(F32)
                sels.append(rank < MOBA_TOPK)

            s_own = jnp.where(causal, s_cur[pl.ds(G + n * BS, BS), cs], NEG)
            m = jnp.max(s_own, axis=0, keepdims=True)
            for j in range(n):
                m_blk = jnp.max(s_cur[pl.ds(G + j * BS, BS), cs], axis=0, keepdims=True)
                m = jnp.maximum(m, jnp.where(sels[j], m_blk, NEG))

            p_cur[pl.ds(n * BS, BS), cs] = jnp.exp2(s_own - m).astype(BF16)
            for j in range(n):
                p = jnp.exp2(s_cur[pl.ds(G + j * BS, BS), cs] - jnp.where(sels[j], m, -NEG))
                p_cur[pl.ds(j * BS, BS), cs] = p.astype(BF16)
            for j in range(n + 1, (t + 1) * QT):
                p_cur[pl.ds(j * BS, BS), cs] = jnp.zeros((BS, BS), BF16)

        acc = jnp.dot(vt_ref[:, 0:nk], p_cur[0:nk, :], preferred_element_type=F32)
        for hf in range(QT):
            cs = slice(hf * BS, (hf + 1) * BS)
            out = acc[0:dh, cs] * (1.0 / acc[dh:dh + 1, cs])
            o_ref[pl.ds((t * QT + hf) * BS, BS), col0:col0 + dh] = out.T.astype(o_ref.dtype)
        yield


def _moba(qkv, batch, seq):
    H, DH, HP = MOBA_HEADS, MOBA_DH, MOBA_HEADS_PER_STEP
    assert 2 * (seq // MOBA_BLOCK) == BF16_SUBLANES, "hi and lo key-sum rows must fill one bf16 tile"
    groups = H // HP
    per_head = [pltpu.VMEM((BF16_SUBLANES + seq, DH), BF16),
                pltpu.VMEM((BF16_SUBLANES, DH), F32),
                pltpu.VMEM((DH + BF16_SUBLANES, seq), BF16),
                *[pltpu.VMEM((BF16_SUBLANES + seq, 2 * MOBA_BLOCK), F32)] * MOBA_SCORE_BUFFERS,
                *[pltpu.VMEM((seq, 2 * MOBA_BLOCK), BF16)] * MOBA_SCORE_BUFFERS]
    assert len(per_head) == MOBA_SCRATCH_PER_HEAD
    return pl.pallas_call(
        _moba_kernel,
        grid=(batch, groups),
        in_specs=[
            pl.BlockSpec((None, HP, seq, DH), lambda b, g: (b, g, 0, 0)),
            pl.BlockSpec((None, HP, seq, DH), lambda b, g: (b, groups + g, 0, 0)),
            pl.BlockSpec((None, HP, seq, DH), lambda b, g: (b, 2 * groups + g, 0, 0)),
        ],
        out_specs=pl.BlockSpec((None, seq, HP * DH), lambda b, g: (b, 0, g)),
        out_shape=jax.ShapeDtypeStruct((batch, seq, H * DH), BF16),
        scratch_shapes=per_head * HP,
        compiler_params=pltpu.CompilerParams(
            dimension_semantics=("arbitrary", "arbitrary"), vmem_limit_bytes=VMEM_LIMIT),
        name="moba",
    )(qkv, qkv, qkv)


def _outproj_mlp_kernel(a_ref, h_ref, wo_ref, gpost_ref, gpre_ref, gmpost_ref, wup_ref, wdn_ref,
                        o_ref, xn_ref, hid_ref, *, sub, ff_chunk):
    assert sum(sub) == a_ref.shape[0]
    subs = [pl.ds(sum(sub[:i]), sub[i]) for i in range(len(sub))]
    for rows in subs:
        mix = jnp.dot(a_ref[rows, :], wo_ref[...], preferred_element_type=F32)
        h1 = h_ref[rows, :] + _rms_norm(mix, gpost_ref[...])
        o_ref[rows, :] = h1
        xn_ref[rows, :] = _rms_norm(h1, gpre_ref[...]).astype(BF16)
    for rows in subs:
        for j in range(wup_ref.shape[1] // ff_chunk):
            cols = pl.ds(j * ff_chunk, ff_chunk)
            u = jnp.maximum(jnp.dot(xn_ref[rows, :], wup_ref[:, cols], preferred_element_type=F32), 0.0)
            hid_ref[rows, cols] = (u * u).astype(BF16)
    for rows in subs:
        dn = jnp.dot(hid_ref[rows, :], wdn_ref[...], preferred_element_type=F32)
        o_ref[rows, :] = o_ref[rows, :] + _rms_norm(dn, gmpost_ref[...])


def _outproj_mlp(a2d, h2d, w_out, g_post, g_mpre, g_mpost, w_up, w_down, *, tm=1024, sub=(256, 256, 256, 256), ff_chunk=1024):
    m_total, d = h2d.shape
    ka = a2d.shape[1]
    dff = w_up.shape[1]
    return pl.pallas_call(
        functools.partial(_outproj_mlp_kernel, sub=sub, ff_chunk=ff_chunk),
        grid=(m_total // tm,),
        in_specs=[
            pl.BlockSpec((tm, ka), lambda m: (m, 0)),
            pl.BlockSpec((tm, d), lambda m: (m, 0)),
            _resident((ka, d)),
            _resident((1, d)),
            _resident((1, d)),
            _resident((1, d)),
            _resident((d, dff)),
            _resident((dff, d)),
        ],
        out_specs=pl.BlockSpec((tm, d), lambda m: (m, 0)),
        out_shape=jax.ShapeDtypeStruct((m_total, d), F32),
        scratch_shapes=[pltpu.VMEM((tm, d), BF16), pltpu.VMEM((tm, dff), BF16)],
        compiler_params=pltpu.CompilerParams(
            dimension_semantics=("arbitrary",), vmem_limit_bytes=VMEM_LIMIT),
        name=f"outproj_mlp_k{ka}",
    )(a2d, h2d, w_out, g_post, g_mpre, g_mpost, w_up, w_down)


def _ret_tables(seq):
    freq = 1.0 / (RET_THETA ** jnp.linspace(0.0, 1.0, RET_DK // 2, dtype=F32))
    ang = jnp.arange(seq)[:, None].astype(F32) * freq[None, :]
    return jnp.cos(ang), jnp.sin(ang)


def _moba_tables(seq):
    half = ROPE_DIM // 2
    inv = ROPE_THETA ** (-jnp.arange(0, ROPE_DIM, 2, dtype=F32) / ROPE_DIM)
    ang = jnp.arange(seq)[:, None].astype(F32) * inv[None, :]
    cos, sin = jnp.cos(ang), jnp.sin(ang)
    gap = LANES // 2 - half
    c = jnp.concatenate([cos, jnp.ones((seq, gap), F32), cos, jnp.ones((seq, gap), F32)], axis=1)
    s = jnp.concatenate([-sin, jnp.zeros((seq, gap), F32), sin, jnp.zeros((seq, gap), F32)], axis=1)
    return c, s


def kernel(x, ln_mix_pre_0, ln_mix_post_0, w_in_0, w_out_0, ln_mlp_pre_0, ln_mlp_post_0, w_up_0, w_down_0,
           ln_mix_pre_1, ln_mix_post_1, w_in_1, w_out_1, ln_mlp_pre_1, ln_mlp_post_1, w_up_1, w_down_1):
    batch, seq, d = x.shape
    row = lambda g: g.reshape(1, d).astype(F32)

    log_gamma = jnp.log(1.0 - 2.0 ** (-5.0 - jnp.arange(RET_HEADS, dtype=F32)))

    h = x.reshape(batch * seq, d)

    (qk, vg), (w_out_0b, w_up_0b, w_down_0b, w_in_1b) = _inproj(
        h, row(ln_mix_pre_0), w_in_0.astype(BF16), _ret_tables(seq), mode="ret", batch=batch, seq=seq,
        cast=(w_out_0, w_up_0, w_down_0, w_in_1))
    mix = _retention(qk, vg, log_gamma, batch, seq)
    h = _outproj_mlp(mix.reshape(batch * seq, -1), h, w_out_0b, row(ln_mix_post_0),
                     row(ln_mlp_pre_0), row(ln_mlp_post_0), w_up_0b, w_down_0b)

    (qkv,), (w_out_1b, w_up_1b, w_down_1b) = _inproj(
        h, row(ln_mix_pre_1), w_in_1b, _moba_tables(seq), mode="moba", batch=batch, seq=seq,
        cast=(w_out_1, w_up_1, w_down_1))
    mix = _moba(qkv, batch, seq)
    h = _outproj_mlp(mix.reshape(batch * seq, -1), h, w_out_1b, row(ln_mix_post_1),
                     row(ln_mlp_pre_1), row(ln_mlp_post_1), w_up_1b, w_down_1b)
    return h.reshape(batch, seq, d)
```

```python
import functools
import math

import jax
import jax.numpy as jnp
from jax import lax
from jax.experimental import pallas as pl
from jax.experimental.pallas import tpu as pltpu

NORM_EPS = 1e-6
NEG = -1e30
LOG2E = math.log2(math.e)

RET_HEADS = 4
RET_DK = 256
RET_DV = 512
RET_THETA = 10000.0
RET_CHUNK = 256

MOBA_HEADS = 8
MOBA_DH = 128
MOBA_BLOCK = 256
MOBA_TOPK = 3
ROPE_THETA = 500000.0
ROPE_DIM = 32

LANES = 128
BF16_SUBLANES = 16
VMEM_LIMIT = 60 * 1024 * 1024

F32 = jnp.float32
BF16 = jnp.bfloat16
NT_DIMS = (((1,), (1,)), ((), ()))


def _rms_norm(x, g):
    return x * lax.rsqrt(jnp.mean(x * x, axis=-1, keepdims=True) + NORM_EPS) * g


def _resident(shape):
    return pl.BlockSpec(shape, lambda *_: (0,) * len(shape), pipeline_mode=pl.Buffered(1))


def _inproj_outputs(mode):
    if mode == "ret":
        return [(2 * RET_HEADS, RET_DK), (2 * RET_HEADS, RET_DV)]
    return [(3 * MOBA_HEADS, MOBA_DH)]


def _inproj_plan(mode):
    if mode == "ret":
        hq = RET_HEADS * RET_DK
        hv = RET_HEADS * RET_DV
        plan = [(1, hv + c * RET_DV, RET_DV, "swish", 1, RET_HEADS + c) for c in range(RET_HEADS)]
        plan += [(0, h * RET_DK, RET_DK, 1.0, 0, h) for h in range(RET_HEADS)]
        plan += [(0, hq + h * RET_DK, RET_DK, RET_DK ** -0.5, 0, RET_HEADS + h) for h in range(RET_HEADS)]
        plan += [(1, c * RET_DV, RET_DV, None, 1, c) for c in range(RET_HEADS)]
    else:
        hq = MOBA_HEADS * MOBA_DH
        per = 4
        cw = per * MOBA_DH
        plan = [(0, c * cw, cw, MOBA_DH ** -0.5 * LOG2E, 0, c * per) for c in range(hq // cw)]
        plan += [(0, hq + c * cw, cw, 1.0, 0, MOBA_HEADS + c * per) for c in range(hq // cw)]
        plan += [(1, c * cw, cw, None, 0, 2 * MOBA_HEADS + c * per) for c in range(hq // cw)]
    return plan


PERM_W = 256
NORM_PIECE = 256


def _qk_column_permutation(mode):
    new = lax.broadcasted_iota(jnp.int32, (PERM_W, PERM_W), 1)
    old = lax.broadcasted_iota(jnp.int32, (PERM_W, PERM_W), 0)
    if mode == "ret":
        half = RET_DK // 2
        src = jnp.where(new < half, 2 * new, 2 * (new - half) + 1)
    else:
        j = new % MOBA_DH
        h16, mid = ROPE_DIM // 2, MOBA_DH // 2
        src_j = jnp.where(j < h16, j, jnp.where(j < mid, j + h16, jnp.where(j < mid + h16, j - (mid - h16), j)))
        src = (new - j) + src_j
    return (old == src).astype(BF16)


def _inproj_kernel(x_ref, g_ref, w_ref, cos_ref, sin_ref, *rest, mode, n_cast):
    cast_in, rest = rest[:n_cast], rest[n_cast:]
    n_out = len(rest) - n_cast - 2
    out_refs, cast_out, (xs_ref, wqk_ref) = rest[:n_out], rest[n_out:n_out + n_cast], rest[n_out + n_cast:]
    n_qk = wqk_ref.shape[1]
    for src, dst in zip(cast_in, cast_out):
        dst[...] = src[...].astype(dst.dtype)

    @pl.when(pl.program_id(0) == 0)
    def _():
        perm = _qk_column_permutation(mode)
        for c0 in range(0, n_qk, PERM_W):
            cols = slice(c0, c0 + PERM_W)
            wqk_ref[:, cols] = jnp.dot(w_ref[:, cols], perm, preferred_element_type=F32).astype(BF16)

    tm = x_ref.shape[0]
    pieces = [pl.ds(r, NORM_PIECE) for r in range(0, tm, NORM_PIECE)]
    for rows in pieces:
        xs_ref[rows, :] = _rms_norm(x_ref[rows, :], g_ref[...]).astype(BF16)
    scaled = {}
    for idx, (wi, start, width, scale, oi, slab0) in enumerate(_inproj_plan(mode)):
        o_ref = out_refs[oi]
        sw = o_ref.shape[-1]
        w_cols = wqk_ref[:, start:start + width] if wi == 0 else w_ref[:, n_qk + start:n_qk + start + width]
        for rows in (pieces if idx == 0 else [pl.ds(0, tm)]):
            acc = jnp.dot(xs_ref[rows, :], w_cols, preferred_element_type=F32)
            if scale is None or scale == "swish":
                if scale == "swish":
                    half = 0.5 * acc
                    acc = half + half * jnp.tanh(half)
                for i in range(width // sw):
                    o_ref[0, slab0 + i, rows, :] = acc[:, i * sw:(i + 1) * sw].astype(o_ref.dtype)
                continue
            if (scale, rows.start, rows.size) not in scaled:
                scaled[(scale, rows.start, rows.size)] = (cos_ref[rows, :] * scale, sin_ref[rows, :] * scale)
            c, s = scaled[(scale, rows.start, rows.size)]
            if mode == "ret":
                x1, x2 = acc[:, :LANES], acc[:, LANES:]
                o_ref[0, slab0, rows, 0:LANES] = (x1 * c - x2 * s).astype(o_ref.dtype)
                o_ref[0, slab0, rows, LANES:width] = (x1 * s + x2 * c).astype(o_ref.dtype)
            else:
                for i in range(width // sw):
                    xh = acc[:, i * sw:(i + 1) * sw]
                    y = xh * c + pltpu.roll(xh, LANES // 2, 1) * s
                    o_ref[0, slab0 + i, rows, :] = y.astype(o_ref.dtype)


def _inproj(x2d, gain, w, tabs, *, mode, batch, seq, cast=(), tm=1024):
    m_total, d = x2d.shape
    n_qk = 2 * (RET_HEADS * RET_DK if mode == "ret" else MOBA_HEADS * MOBA_DH)
    tiles_per_seq = seq // tm
    steps = m_total // tm
    tab_spec = pl.BlockSpec((tm, LANES), lambda m: (m % tiles_per_seq, 0))
    outs = _inproj_outputs(mode)
    for a in cast:
        assert a.shape[0] % (steps * BF16_SUBLANES) == 0, a.shape
    cast_specs = [pl.BlockSpec((a.shape[0] // steps, a.shape[1]), lambda m: (m, 0)) for a in cast]
    res = pl.pallas_call(
        functools.partial(_inproj_kernel, mode=mode, n_cast=len(cast)),
        grid=(steps,),
        in_specs=[pl.BlockSpec((tm, d), lambda m: (m, 0)), _resident((1, d)), _resident(w.shape),
                  tab_spec, tab_spec, *cast_specs],
        out_specs=[pl.BlockSpec((1, ns, tm, sw), lambda m: (m // tiles_per_seq, 0, m % tiles_per_seq, 0))
                   for ns, sw in outs] + cast_specs,
        out_shape=[jax.ShapeDtypeStruct((batch, ns, seq, sw), BF16) for ns, sw in outs]
        + [jax.ShapeDtypeStruct(a.shape, BF16) for a in cast],
        scratch_shapes=[pltpu.VMEM((tm, d), BF16), pltpu.VMEM((d, n_qk), BF16)],
        compiler_params=pltpu.CompilerParams(
            dimension_semantics=("arbitrary",), vmem_limit_bytes=VMEM_LIMIT),
        name=f"inproj_{mode}",
    )(x2d, gain, w, *tabs, *cast)
    return res[:len(outs)], res[len(outs):]


RET_HEADS_PER_STEP = 2


def _retention_decays(lg):
    L = RET_CHUNK
    ii = lax.broadcasted_iota(jnp.int32, (L, L), 0)
    jj = lax.broadcasted_iota(jnp.int32, (L, L), 1)
    diff = (ii - jj).astype(F32)
    d_intra = jnp.where(diff >= 0, jnp.exp(lg * jnp.maximum(diff, 0.0)), 0.0)
    col = lax.broadcasted_iota(jnp.int32, (L, 1), 0).astype(F32)
    q_decay = jnp.exp(lg * (col + 1.0))
    row = lax.broadcasted_iota(jnp.int32, (1, L), 1).astype(F32)
    k_decay = jnp.exp(lg * (L - 1.0 - row))
    chunk_decay = jnp.exp(jnp.full((1, 1), lg * L, F32))
    return d_intra, q_decay, k_decay, chunk_decay


def _retention_kernel(lg_ref, q_ref, k_ref, v_ref, g_ref, o_ref, state_ref):
    L = RET_CHUNK
    hp, seq, _ = q_ref.shape
    dv = v_ref.shape[2]
    head0 = pl.program_id(1) * hp
    decays = [_retention_decays(lg_ref[head0 + i]) for i in range(hp)]

    state_ref[...] = jnp.zeros_like(state_ref)
    for c in range(seq // L):
        rows = pl.ds(c * L, L)
        for i in range(hp):
            d_intra, q_decay, k_decay, chunk_decay = decays[i]
            qc = q_ref[i, rows, :]
            kc = k_ref[i, rows, :]
            vc = v_ref[i, rows, :]
            scores = lax.dot_general(qc, kc, NT_DIMS, preferred_element_type=F32) * d_intra
            o = jnp.dot(scores.astype(BF16), vc, preferred_element_type=F32)
            state = state_ref[i]
            o = o + jnp.dot(qc, state.astype(BF16), preferred_element_type=F32) * q_decay
            kd_t = (kc.astype(F32).T * k_decay).astype(BF16)
            state_ref[i] = state * chunk_decay + jnp.dot(kd_t, vc, preferred_element_type=F32)
            o = o * lax.rsqrt(jnp.mean(o * o, axis=-1, keepdims=True) + NORM_EPS)
            o_ref[rows, i * dv:(i + 1) * dv] = (o * g_ref[i, rows, :].astype(F32)).astype(o_ref.dtype)


def _retention(qk, vg, log_gamma, batch, seq):
    H, DK, DV, HP = RET_HEADS, RET_DK, RET_DV, RET_HEADS_PER_STEP
    groups = H // HP
    grid_spec = pltpu.PrefetchScalarGridSpec(
        num_scalar_prefetch=1,
        grid=(batch, groups),
        in_specs=[
            pl.BlockSpec((None, HP, seq, DK), lambda b, g, lg: (b, g, 0, 0)),
            pl.BlockSpec((None, HP, seq, DK), lambda b, g, lg: (b, groups + g, 0, 0)),
            pl.BlockSpec((None, HP, seq, DV), lambda b, g, lg: (b, g, 0, 0)),
            pl.BlockSpec((None, HP, seq, DV), lambda b, g, lg: (b, groups + g, 0, 0)),
        ],
        out_specs=pl.BlockSpec((None, seq, HP * DV), lambda b, g, lg: (b, 0, g)),
        scratch_shapes=[pltpu.VMEM((HP, DK, DV), F32)],
    )
    return pl.pallas_call(
        _retention_kernel,
        grid_spec=grid_spec,
        out_shape=jax.ShapeDtypeStruct((batch, seq, H * DV), BF16),
        compiler_params=pltpu.CompilerParams(
            dimension_semantics=("arbitrary", "arbitrary"), vmem_limit_bytes=VMEM_LIMIT),
        name="retention",
    )(log_gamma, qk, qk, vg, vg)


MOBA_HEADS_PER_STEP = 2
MOBA_SCORE_BUFFERS = 2
MOBA_SCRATCH_PER_HEAD = 3 + 2 * MOBA_SCORE_BUFFERS


def _moba_kernel(q_ref, k_ref, v_ref, o_ref, *scratch):
    dh = q_ref.shape[2]
    heads = []
    for i in range(q_ref.shape[0]):
        bufs = scratch[i * MOBA_SCRATCH_PER_HEAD:(i + 1) * MOBA_SCRATCH_PER_HEAD]
        heads.append(_moba_head(q_ref.at[i], k_ref.at[i], v_ref.at[i], o_ref, i * dh, *bufs))
    done = object()
    while heads:
        heads = [g for g in heads if next(g, done) is not done]


def _moba_head(q_ref, k_ref, v_ref, o_ref, col0, ka_ref, ksum_ref, vt_ref, *sp_refs):
    nbuf = len(sp_refs) // 2
    s_bufs, p_bufs = sp_refs[:nbuf], sp_refs[nbuf:]
    BS = MOBA_BLOCK
    seq, dh = q_ref.shape
    nb = seq // BS
    G = ka_ref.shape[0] - seq

    for j in range(nb):
        ksum = jnp.sum(k_ref[pl.ds(j * BS, BS), :].astype(F32), axis=0, keepdims=True)
        hi = ksum.astype(BF16).astype(F32)
        ksum_ref[j:j + 1, :] = hi
        ksum_ref[nb + j:nb + j + 1, :] = ksum - hi
    ka_ref[0:G, :] = ksum_ref[...].astype(BF16)
    ka_ref[G:, :] = k_ref[...]
    gate_floor = NEG * BS * LOG2E

    QT = 2
    order = list(range(nb // QT - 1, -1, -1))

    def scores(i):
        t = order[i]
        nr = G + (t + 1) * QT * BS
        s_bufs[i % nbuf][0:nr, :] = lax.dot_general(ka_ref[0:nr, :], q_ref[pl.ds(t * QT * BS, QT * BS), :],
                                                 NT_DIMS, preferred_element_type=F32)

    scores(0)

    for j in range(nb):
        cols = pl.ds(j * BS, BS)
        vt_ref[0:dh, cols] = v_ref[cols, :].astype(F32).T.astype(BF16)
    vt_ref[dh:, :] = jnp.ones((vt_ref.shape[0] - dh, seq), BF16)

    key_idx = lax.broadcasted_iota(jnp.int32, (BS, BS), 0)
    qry_idx = lax.broadcasted_iota(jnp.int32, (BS, BS), 1)
    causal = key_idx <= qry_idx
    yield

    for i, t in enumerate(order):
        nk = (t + 1) * QT * BS
        if nbuf > 1 and i + 1 < len(order):
            scores(i + 1)
        if nbuf == 1 and i > 0:
            scores(i)
        s_cur, p_cur = s_bufs[i % nbuf], p_bufs[i % nbuf]

        for hf in range(QT):
            n = t * QT + hf
            cs = pl.ds(hf * BS, BS)

            gate = [s_cur[j:j + 1, cs] + s_cur[nb + j:nb + j + 1, cs] for j in range(n)]

            sels = []
            for j in range(n):
                rank = jnp.where(gate[j] < gate_floor, float(nb - n), 0.0)
                for r in range(n):
                    if r != j:
                        ahead = (gate[r] >= gate[j]) if r < j else (gate[r] > gate[j])
                        rank = rank + ahead.astype(F32)
                sels.append(rank < MOBA_TOPK)

            s_own = jnp.where(causal, s_cur[pl.ds(G + n * BS, BS), cs], NEG)
            m = jnp.max(s_own, axis=0, keepdims=True)
            for j in range(n):
                m_blk = jnp.max(s_cur[pl.ds(G + j * BS, BS), cs], axis=0, keepdims=True)
                m = jnp.maximum(m, jnp.where(sels[j], m_blk, NEG))

            p_cur[pl.ds(n * BS, BS), cs] = jnp.exp2(s_own - m).astype(BF16)
            for j in range(n):
                p = jnp.exp2(s_cur[pl.ds(G + j * BS, BS), cs] - jnp.where(sels[j], m, -NEG))
                p_cur[pl.ds(j * BS, BS), cs] = p.astype(BF16)
            for j in range(n + 1, (t + 1) * QT):
                p_cur[pl.ds(j * BS, BS), cs] = jnp.zeros((BS, BS), BF16)

        acc = jnp.dot(vt_ref[:, 0:nk], p_cur[0:nk, :], preferred_element_type=F32)
        for hf in range(QT):
            cs = slice(hf * BS, (hf + 1) * BS)
            out = acc[0:dh, cs] * (1.0 / acc[dh:dh + 1, cs])
            o_ref[pl.ds((t * QT + hf) * BS, BS), col0:col0 + dh] = out.T.astype(o_ref.dtype)
        yield


def _moba(qkv, batch, seq):
    H, DH, HP = MOBA_HEADS, MOBA_DH, MOBA_HEADS_PER_STEP
    assert 2 * (seq // MOBA_BLOCK) == BF16_SUBLANES, "hi and lo key-sum rows must fill one bf16 tile"
    groups = H // HP
    per_head = [pltpu.VMEM((BF16_SUBLANES + seq, DH), BF16),
                pltpu.VMEM((BF16_SUBLANES, DH), F32),
                pltpu.VMEM((DH + BF16_SUBLANES, seq), BF16),
                *[pltpu.VMEM((BF16_SUBLANES + seq, 2 * MOBA_BLOCK), F32)] * MOBA_SCORE_BUFFERS,
                *[pltpu.VMEM((seq, 2 * MOBA_BLOCK), BF16)] * MOBA_SCORE_BUFFERS]
    assert len(per_head) == MOBA_SCRATCH_PER_HEAD
    return pl.pallas_call(
        _moba_kernel,
        grid=(batch, groups),
        in_specs=[
            pl.BlockSpec((None, HP, seq, DH), lambda b, g: (b, g, 0, 0)),
            pl.BlockSpec((None, HP, seq, DH), lambda b, g: (b, groups + g, 0, 0)),
            pl.BlockSpec((None, HP, seq, DH), lambda b, g: (b, 2 * groups + g, 0, 0)),
        ],
        out_specs=pl.BlockSpec((None, seq, HP * DH), lambda b, g: (b, 0, g)),
        out_shape=jax.ShapeDtypeStruct((batch, seq, H * DH), BF16),
        scratch_shapes=per_head * HP,
        compiler_params=pltpu.CompilerParams(
            dimension_semantics=("arbitrary", "arbitrary"), vmem_limit_bytes=VMEM_LIMIT),
        name="moba",
    )(qkv, qkv, qkv)


def _outproj_mlp_kernel(a_ref, h_ref, wo_ref, gpost_ref, gpre_ref, gmpost_ref, wup_ref, wdn_ref,
                        o_ref, xn_ref, hid_ref, *, sub, ff_chunk):
    assert sum(sub) == a_ref.shape[0]
    subs = [pl.ds(sum(sub[:i]), sub[i]) for i in range(len(sub))]
    for rows in subs:
        mix = jnp.dot(a_ref[rows, :], wo_ref[...], preferred_element_type=F32)
        h1 = h_ref[rows, :] + _rms_norm(mix, gpost_ref[...])
        o_ref[rows, :] = h1
        xn_ref[rows, :] = _rms_norm(h1, gpre_ref[...]).astype(BF16)
    for rows in subs:
        for j in range(wup_ref.shape[1] // ff_chunk):
            cols = pl.ds(j * ff_chunk, ff_chunk)
            u = jnp.maximum(jnp.dot(xn_ref[rows, :], wup_ref[:, cols], preferred_element_type=F32), 0.0)
            hid_ref[rows, cols] = (u * u).astype(BF16)
    for rows in subs:
        dn = jnp.dot(hid_ref[rows, :], wdn_ref[...], preferred_element_type=F32)
        o_ref[rows, :] = o_ref[rows, :] + _rms_norm(dn, gmpost_ref[...])


def _outproj_mlp(a2d, h2d, w_out, g_post, g_mpre, g_mpost, w_up, w_down, *, tm=1024, sub=(256, 256, 256, 256), ff_chunk=1024):
    m_total, d = h2d.shape
    ka = a2d.shape[1]
    dff = w_up.shape[1]
    return pl.pallas_call(
        functools.partial(_outproj_mlp_kernel, sub=sub, ff_chunk=ff_chunk),
        grid=(m_total // tm,),
        in_specs=[
            pl.BlockSpec((tm, ka), lambda m: (m, 0)),
            pl.BlockSpec((tm, d), lambda m: (m, 0)),
            _resident((ka, d)),
            _resident((1, d)),
            _resident((1, d)),
            _resident((1, d)),
            _resident((d, dff)),
            _resident((dff, d)),
        ],
        out_specs=pl.BlockSpec((tm, d), lambda m: (m, 0)),
        out_shape=jax.ShapeDtypeStruct((m_total, d), F32),
        scratch_shapes=[pltpu.VMEM((tm, d), BF16), pltpu.VMEM((tm, dff), BF16)],
        compiler_params=pltpu.CompilerParams(
            dimension_semantics=("arbitrary",), vmem_limit_bytes=VMEM_LIMIT),
        name=f"outproj_mlp_k{ka}",
    )(a2d, h2d, w_out, g_post, g_mpre, g_mpost, w_up, w_down)


def _ret_tables(seq):
    freq = 1.0 / (RET_THETA ** jnp.linspace(0.0, 1.0, RET_DK // 2, dtype=F32))
    ang = jnp.arange(seq)[:, None].astype(F32) * freq[None, :]
    return jnp.cos(ang), jnp.sin(ang)


def _moba_tables(seq):
    half = ROPE_DIM // 2
    inv = ROPE_THETA ** (-jnp.arange(0, ROPE_DIM, 2, dtype=F32) / ROPE_DIM)
    ang = jnp.arange(seq)[:, None].astype(F32) * inv[None, :]
    cos, sin = jnp.cos(ang), jnp.sin(ang)
    gap = LANES // 2 - half
    c = jnp.concatenate([cos, jnp.ones((seq, gap), F32), cos, jnp.ones((seq, gap), F32)], axis=1)
    s = jnp.concatenate([-sin, jnp.zeros((seq, gap), F32), sin, jnp.zeros((seq, gap), F32)], axis=1)
    return c, s


def kernel(x, ln_mix_pre_0, ln_mix_post_0, w_in_0, w_out_0, ln_mlp_pre_0, ln_mlp_post_0, w_up_0, w_down_0,
           ln_mix_pre_1, ln_mix_post_1, w_in_1, w_out_1, ln_mlp_pre_1, ln_mlp_post_1, w_up_1, w_down_1):
    batch, seq, d = x.shape
    row = lambda g: g.reshape(1, d).astype(F32)

    log_gamma = jnp.log(1.0 - 2.0 ** (-5.0 - jnp.arange(RET_HEADS, dtype=F32)))

    h = x.reshape(batch * seq, d)

    (qk, vg), (w_out_0b, w_up_0b, w_down_0b, w_in_1b) = _inproj(
        h, row(ln_mix_pre_0), w_in_0.astype(BF16), _ret_tables(seq), mode="ret", batch=batch, seq=seq,
        cast=(w_out_0, w_up_0, w_down_0, w_in_1))
    mix = _retention(qk, vg, log_gamma, batch, seq)
    h = _outproj_mlp(mix.reshape(batch * seq, -1), h, w_out_0b, row(ln_mix_post_0),
                     row(ln_mlp_pre_0), row(ln_mlp_post_0), w_up_0b, w_down_0b)

    (qkv,), (w_out_1b, w_up_1b, w_down_1b) = _inproj(
        h, row(ln_mix_pre_1), w_in_1b, _moba_tables(seq), mode="moba", batch=batch, seq=seq,
        cast=(w_out_1, w_up_1, w_down_1))
    mix = _moba(qkv, batch, seq)
    h = _outproj_mlp(mix.reshape(batch * seq, -1), h, w_out_1b, row(ln_mix_post_1),
                     row(ln_mlp_pre_1), row(ln_mlp_post_1), w_up_1b, w_down_1b)
    return h.reshape(batch, seq, d)
```

```python
import functools
import math

import jax
import jax.numpy as jnp
from jax import lax
from jax.experimental import pallas as pl
from jax.experimental.pallas import tpu as pltpu

NORM_EPS = 1e-6
NEG = -1e30
LOG2E = math.log2(math.e)

RET_HEADS = 4
RET_DK = 256
RET_DV = 512
RET_THETA = 10000.0
RET_CHUNK = 256

MOBA_HEADS = 8
MOBA_DH = 128
MOBA_BLOCK = 256
MOBA_TOPK = 3
ROPE_THETA = 500000.0
ROPE_DIM = 32

LANES = 128
BF16_SUBLANES = 16
VMEM_LIMIT = 60 * 1024 * 1024

F32 = jnp.float32
BF16 = jnp.bfloat16
NT_DIMS = (((1,), (1,)), ((), ()))


def _rms_norm(x, g):
    return x * lax.rsqrt(jnp.mean(x * x, axis=-1, keepdims=True) + NORM_EPS) * g


def _resident(shape):
    return pl.BlockSpec(shape, lambda *_: (0,) * len(shape), pipeline_mode=pl.Buffered(1))


def _inproj_outputs(mode):
    if mode == "ret":
        return [(2 * RET_HEADS, RET_DK), (2 * RET_HEADS, RET_DV)]
    return [(3 * MOBA_HEADS, MOBA_DH)]


def _inproj_plan(mode):
    if mode == "ret":
        hq = RET_HEADS * RET_DK
        hv = RET_HEADS * RET_DV
        plan = [(1, hv + c * RET_DV, RET_DV, "swish", 1, RET_HEADS + c) for c in range(RET_HEADS)]
        plan += [(0, h * RET_DK, RET_DK, 1.0, 0, h) for h in range(RET_HEADS)]
        plan += [(0, hq + h * RET_DK, RET_DK, RET_DK ** -0.5, 0, RET_HEADS + h) for h in range(RET_HEADS)]
        plan += [(1, c * RET_DV, RET_DV, None, 1, c) for c in range(RET_HEADS)]
    else:
        hq = MOBA_HEADS * MOBA_DH
        per = 4
        cw = per * MOBA_DH
        plan = [(0, c * cw, cw, MOBA_DH ** -0.5 * LOG2E, 0, c * per) for c in range(hq // cw)]
        plan += [(0, hq + c * cw, cw, 1.0, 0, MOBA_HEADS + c * per) for c in range(hq // cw)]
        plan += [(1, c * cw, cw, None, 0, 2 * MOBA_HEADS + c * per) for c in range(hq // cw)]
    return plan


PERM_W = 256
NORM_PIECE = 256


def _qk_column_permutation(mode):
    new = lax.broadcasted_iota(jnp.int32, (PERM_W, PERM_W), 1)
    old = lax.broadcasted_iota(jnp.int32, (PERM_W, PERM_W), 0)
    if mode == "ret":
        half = RET_DK // 2
        src = jnp.where(new < half, 2 * new, 2 * (new - half) + 1)
    else:
        j = new % MOBA_DH
        h16, mid = ROPE_DIM // 2, MOBA_DH // 2
        src_j = jnp.where(j < h16, j, jnp.where(j < mid, j + h16, jnp.where(j < mid + h16, j - (mid - h16), j)))
        src = (new - j) + src_j
    return (old == src).astype(BF16)


def _inproj_kernel(x_ref, g_ref, w_ref, cos_ref, sin_ref, *rest, mode, n_cast):
    cast_in, rest = rest[:n_cast], rest[n_cast:]
    n_out = len(rest) - n_cast - 2
    out_refs, cast_out, (xs_ref, wqk_ref) = rest[:n_out], rest[n_out:n_out + n_cast], rest[n_out + n_cast:]
    n_qk = wqk_ref.shape[1]
    for src, dst in zip(cast_in, cast_out):
        dst[...] = src[...].astype(dst.dtype)

    @pl.when(pl.program_id(0) == 0)
    def _():
        perm = _qk_column_permutation(mode)
        for c0 in range(0, n_qk, PERM_W):
            cols = slice(c0, c0 + PERM_W)
            wqk_ref[:, cols] = jnp.dot(w_ref[:, cols], perm, preferred_element_type=F32).astype(BF16)

    tm = x_ref.shape[0]
    pieces = [pl.ds(r, NORM_PIECE) for r in range(0, tm, NORM_PIECE)]
    for rows in pieces:
        xs_ref[rows, :] = _rms_norm(x_ref[rows, :], g_ref[...]).astype(BF16)
    scaled = {}
    for idx, (wi, start, width, scale, oi, slab0) in enumerate(_inproj_plan(mode)):
        o_ref = out_refs[oi]
        sw = o_ref.shape[-1]
        w_cols = wqk_ref[:, start:start + width] if wi == 0 else w_ref[:, n_qk + start:n_qk + start + width]
        for rows in (pieces if idx == 0 else [pl.ds(0, tm)]):
            acc = jnp.dot(xs_ref[rows, :], w_cols, preferred_element_type=F32)
            if scale is None or scale == "swish":
                if scale == "swish":
                    half = 0.5 * acc
                    acc = half + half * jnp.tanh(half)
                for i in range(width // sw):
                    o_ref[0, slab0 + i, rows, :] = acc[:, i * sw:(i + 1) * sw].astype(o_ref.dtype)
                continue
            if mode == "ret":
                head, is_k = slab0 % RET_HEADS, slab0 >= RET_HEADS
                log_gamma = math.log(1.0 - 2.0 ** (-5.0 - head))
                pos = lax.broadcasted_iota(jnp.int32, (rows.size, 1), 0) + rows.start
                decay = jnp.exp((pos % RET_CHUNK).astype(F32) * (-log_gamma if is_k else log_gamma)) * scale
                c, s = cos_ref[rows, :] * decay, sin_ref[rows, :] * decay
                x1, x2 = acc[:, :LANES], acc[:, LANES:]
                o_ref[0, slab0, rows, 0:LANES] = (x1 * c - x2 * s).astype(o_ref.dtype)
                o_ref[0, slab0, rows, LANES:width] = (x1 * s + x2 * c).astype(o_ref.dtype)
            else:
                if (scale, rows.start, rows.size) not in scaled:
                    scaled[(scale, rows.start, rows.size)] = (cos_ref[rows, :] * scale, sin_ref[rows, :] * scale)
                c, s = scaled[(scale, rows.start, rows.size)]
                for i in range(width // sw):
                    xh = acc[:, i * sw:(i + 1) * sw]
                    y = xh * c + pltpu.roll(xh, LANES // 2, 1) * s
                    o_ref[0, slab0 + i, rows, :] = y.astype(o_ref.dtype)


def _inproj(x2d, gain, w, tabs, *, mode, batch, seq, cast=(), tm=1024):
    m_total, d = x2d.shape
    n_qk = 2 * (RET_HEADS * RET_DK if mode == "ret" else MOBA_HEADS * MOBA_DH)
    tiles_per_seq = seq // tm
    steps = m_total // tm
    tab_spec = pl.BlockSpec((tm, LANES), lambda m: (m % tiles_per_seq, 0))
    outs = _inproj_outputs(mode)
    for a in cast:
        assert a.shape[0] % (steps * BF16_SUBLANES) == 0, a.shape
    cast_specs = [pl.BlockSpec((a.shape[0] // steps, a.shape[1]), lambda m: (m, 0)) for a in cast]
    res = pl.pallas_call(
        functools.partial(_inproj_kernel, mode=mode, n_cast=len(cast)),
        grid=(steps,),
        in_specs=[pl.BlockSpec((tm, d), lambda m: (m, 0)), _resident((1, d)), _resident(w.shape),
                  tab_spec, tab_spec, *cast_specs],
        out_specs=[pl.BlockSpec((1, ns, tm, sw), lambda m: (m // tiles_per_seq, 0, m % tiles_per_seq, 0))
                   for ns, sw in outs] + cast_specs,
        out_shape=[jax.ShapeDtypeStruct((batch, ns, seq, sw), BF16) for ns, sw in outs]
        + [jax.ShapeDtypeStruct(a.shape, BF16) for a in cast],
        scratch_shapes=[pltpu.VMEM((tm, d), BF16), pltpu.VMEM((d, n_qk), BF16)],
        compiler_params=pltpu.CompilerParams(
            dimension_semantics=("arbitrary",), vmem_limit_bytes=VMEM_LIMIT),
        name=f"inproj_{mode}",
    )(x2d, gain, w, *tabs, *cast)
    return res[:len(outs)], res[len(outs):]


RET_HEADS_PER_STEP = 2


def _retention_kernel(lg_ref, q_ref, k_ref, v_ref, g_ref, o_ref, state_ref):
    L = RET_CHUNK
    hp, seq, _ = q_ref.shape
    dv = v_ref.shape[2]
    head0 = pl.program_id(1) * hp
    chunk_decay = [jnp.exp(jnp.full((1, 1), lg_ref[head0 + i] * L, F32)) for i in range(hp)]
    causal = lax.broadcasted_iota(jnp.int32, (L, L), 0) >= lax.broadcasted_iota(jnp.int32, (L, L), 1)

    state_ref[...] = jnp.zeros_like(state_ref)
    for c in range(seq // L):
        rows = pl.ds(c * L, L)
        for i in range(hp):
            qc = q_ref[i, rows, :]
            kc = k_ref[i, rows, :]
            vc = v_ref[i, rows, :]
            scores = jnp.where(causal, lax.dot_general(qc, kc, NT_DIMS, preferred_element_type=F32), 0.0)
            state = state_ref[i]
            o = (jnp.dot(scores.astype(BF16), vc, preferred_element_type=F32)
                 + jnp.dot(qc, state.astype(BF16), preferred_element_type=F32))
            k_t = kc.astype(F32).T.astype(BF16)
            state_ref[i] = (state + jnp.dot(k_t, vc, preferred_element_type=F32)) * chunk_decay[i]
            o = o * lax.rsqrt(jnp.mean(o * o, axis=-1, keepdims=True) + NORM_EPS)
            o_ref[rows, i * dv:(i + 1) * dv] = (o * g_ref[i, rows, :].astype(F32)).astype(o_ref.dtype)


def _retention(qk, vg, log_gamma, batch, seq):
    H, DK, DV, HP = RET_HEADS, RET_DK, RET_DV, RET_HEADS_PER_STEP
    groups = H // HP
    grid_spec = pltpu.PrefetchScalarGridSpec(
        num_scalar_prefetch=1,
        grid=(batch, groups),
        in_specs=[
            pl.BlockSpec((None, HP, seq, DK), lambda b, g, lg: (b, g, 0, 0)),
            pl.BlockSpec((None, HP, seq, DK), lambda b, g, lg: (b, groups + g, 0, 0)),
            pl.BlockSpec((None, HP, seq, DV), lambda b, g, lg: (b, g, 0, 0)),
            pl.BlockSpec((None, HP, seq, DV), lambda b, g, lg: (b, groups + g, 0, 0)),
        ],
        out_specs=pl.BlockSpec((None, seq, HP * DV), lambda b, g, lg: (b, 0, g)),
        scratch_shapes=[pltpu.VMEM((HP, DK, DV), F32)],
    )
    return pl.pallas_call(
        _retention_kernel,
        grid_spec=grid_spec,
        out_shape=jax.ShapeDtypeStruct((batch, seq, H * DV), BF16),
        compiler_params=pltpu.CompilerParams(
            dimension_semantics=("arbitrary", "arbitrary"), vmem_limit_bytes=VMEM_LIMIT),
        name="retention",
    )(log_gamma, qk, qk, vg, vg)


MOBA_HEADS_PER_STEP = 2
MOBA_SCORE_BUFFERS = 2
MOBA_SCRATCH_PER_HEAD = 3 + 2 * MOBA_SCORE_BUFFERS


def _moba_kernel(q_ref, k_ref, v_ref, o_ref, *scratch):
    dh = q_ref.shape[2]
    heads = []
    for i in range(q_ref.shape[0]):
        bufs = scratch[i * MOBA_SCRATCH_PER_HEAD:(i + 1) * MOBA_SCRATCH_PER_HEAD]
        heads.append(_moba_head(q_ref.at[i], k_ref.at[i], v_ref.at[i], o_ref, i * dh, *bufs))
    done = object()
    while heads:
        heads = [g for g in heads if next(g, done) is not done]


def _moba_head(q_ref, k_ref, v_ref, o_ref, col0, ka_ref, ksum_ref, vt_ref, *sp_refs):
    nbuf = len(sp_refs) // 2
    s_bufs, p_bufs = sp_refs[:nbuf], sp_refs[nbuf:]
    BS = MOBA_BLOCK
    seq, dh = q_ref.shape
    nb = seq // BS
    G = ka_ref.shape[0] - seq

    for j in range(nb):
        ksum = jnp.sum(k_ref[pl.ds(j * BS, BS), :].astype(F32), axis=0, keepdims=True)
        hi = ksum.astype(BF16).astype(F32)
        ksum_ref[j:j + 1, :] = hi
        ksum_ref[nb + j:nb + j + 1, :] = ksum - hi
    ka_ref[0:G, :] = ksum_ref[...].astype(BF16)
    ka_ref[G:, :] = k_ref[...]
    gate_floor = NEG * BS * LOG2E

    QT = 2
    order = list(range(nb // QT - 1, -1, -1))

    def scores(i):
        t = order[i]
        nr = G + (t + 1) * QT * BS
        s_bufs[i % nbuf][0:nr, :] = lax.dot_general(ka_ref[0:nr, :], q_ref[pl.ds(t * QT * BS, QT * BS), :],
                                                 NT_DIMS, preferred_element_type=F32)

    scores(0)

    for j in range(nb):
        cols = pl.ds(j * BS, BS)
        vt_ref[0:dh, cols] = v_ref[cols, :].astype(F32).T.astype(BF16)
    vt_ref[dh:, :] = jnp.ones((vt_ref.shape[0] - dh, seq), BF16)

    key_idx = lax.broadcasted_iota(jnp.int32, (BS, BS), 0)
    qry_idx = lax.broadcasted_iota(jnp.int32, (BS, BS), 1)
    causal = key_idx <= qry_idx
    yield

    for i, t in enumerate(order):
        nk = (t + 1) * QT * BS
        if nbuf > 1 and i + 1 < len(order):
            scores(i + 1)
        if nbuf == 1 and i > 0:
            scores(i)
        s_cur, p_cur = s_bufs[i % nbuf], p_bufs[i % nbuf]

        for hf in range(QT):
            n = t * QT + hf
            cs = pl.ds(hf * BS, BS)

            gate = [s_cur[j:j + 1, cs] + s_cur[nb + j:nb + j + 1, cs] for j in range(n)]

            sels = []
            for j in range(n):
                rank = jnp.where(gate[j] < gate_floor, float(nb - n), 0.0)
                for r in range(n):
                    if r != j:
                        ahead = (gate[r] >= gate[j]) if r < j else (gate[r] > gate[j])
                        rank = rank + ahead.astype(F32)
                sels.append(rank < MOBA_TOPK)

            s_own = jnp.where(causal, s_cur[pl.ds(G + n * BS, BS), cs], NEG)
            m = jnp.max(s_own, axis=0, keepdims=True)
            for j in range(n):
                m_blk = jnp.max(s_cur[pl.ds(G + j * BS, BS), cs], axis=0, keepdims=True)
                m = jnp.maximum(m, jnp.where(sels[j], m_blk, NEG))

            p_cur[pl.ds(n * BS, BS), cs] = jnp.exp2(s_own - m).astype(BF16)
            for j in range(n):
                p = jnp.exp2(s_cur[pl.ds(G + j * BS, BS), cs] - jnp.where(sels[j], m, -NEG))
                p_cur[pl.ds(j * BS, BS), cs] = p.astype(BF16)
            for j in range(n + 1, (t + 1) * QT):
                p_cur[pl.ds(j * BS, BS), cs] = jnp.zeros((BS, BS), BF16)

        acc = jnp.dot(vt_ref[:, 0:nk], p_cur[0:nk, :], preferred_element_type=F32)
        for hf in range(QT):
            cs = slice(hf * BS, (hf + 1) * BS)
            out = acc[0:dh, cs] * (1.0 / acc[dh:dh + 1, cs])
            o_ref[pl.ds((t * QT + hf) * BS, BS), col0:col0 + dh] = out.T.astype(o_ref.dtype)
        yield


def _moba(qkv, batch, seq):
    H, DH, HP = MOBA_HEADS, MOBA_DH, MOBA_HEADS_PER_STEP
    assert 2 * (seq // MOBA_BLOCK) == BF16_SUBLANES, "hi and lo key-sum rows must fill one bf16 tile"
    groups = H // HP
    per_head = [pltpu.VMEM((BF16_SUBLANES + seq, DH), BF16),
                pltpu.VMEM((BF16_SUBLANES, DH), F32),
                pltpu.VMEM((DH + BF16_SUBLANES, seq), BF16),
                *[pltpu.VMEM((BF16_SUBLANES + seq, 2 * MOBA_BLOCK), F32)] * MOBA_SCORE_BUFFERS,
                *[pltpu.VMEM((seq, 2 * MOBA_BLOCK), BF16)] * MOBA_SCORE_BUFFERS]
    assert len(per_head) == MOBA_SCRATCH_PER_HEAD
    return pl.pallas_call(
        _moba_kernel,
        grid=(batch, groups),
        in_specs=[
            pl.BlockSpec((None, HP, seq, DH), lambda b, g: (b, g, 0, 0)),
            pl.BlockSpec((None, HP, seq, DH), lambda b, g: (b, groups + g, 0, 0)),
            pl.BlockSpec((None, HP, seq, DH), lambda b, g: (b, 2 * groups + g, 0, 0)),
        ],
        out_specs=pl.BlockSpec((None, seq, HP * DH), lambda b, g: (b, 0, g)),
        out_shape=jax.ShapeDtypeStruct((batch, seq, H * DH), BF16),
        scratch_shapes=per_head * HP,
        compiler_params=pltpu.CompilerParams(
            dimension_semantics=("arbitrary", "arbitrary"), vmem_limit_bytes=VMEM_LIMIT),
        name="moba",
    )(qkv, qkv, qkv)


def _outproj_mlp_kernel(a_ref, h_ref, wo_ref, gpost_ref, gpre_ref, gmpost_ref, wup_ref, wdn_ref,
                        o_ref, xn_ref, hid_ref, *, sub, ff_chunk):
    assert sum(sub) == a_ref.shape[0]
    subs = [pl.ds(sum(sub[:i]), sub[i]) for i in range(len(sub))]
    for rows in subs:
        mix = jnp.dot(a_ref[rows, :], wo_ref[...], preferred_element_type=F32)
        h1 = h_ref[rows, :] + _rms_norm(mix, gpost_ref[...])
        o_ref[rows, :] = h1
        xn_ref[rows, :] = _rms_norm(h1, gpre_ref[...]).astype(BF16)
    for rows in subs:
        for j in range(wup_ref.shape[1] // ff_chunk):
            cols = pl.ds(j * ff_chunk, ff_chunk)
            u = jnp.maximum(jnp.dot(xn_ref[rows, :], wup_ref[:, cols], preferred_element_type=F32), 0.0)
            hid_ref[rows, cols] = (u * u).astype(BF16)
    for rows in subs:
        dn = jnp.dot(hid_ref[rows, :], wdn_ref[...], preferred_element_type=F32)
        o_ref[rows, :] = o_ref[rows, :] + _rms_norm(dn, gmpost_ref[...])


def _outproj_mlp(a2d, h2d, w_out, g_post, g_mpre, g_mpost, w_up, w_down, *, tm=1024, sub=(256, 256, 256, 256), ff_chunk=1024):
    m_total, d = h2d.shape
    ka = a2d.shape[1]
    dff = w_up.shape[1]
    return pl.pallas_call(
        functools.partial(_outproj_mlp_kernel, sub=sub, ff_chunk=ff_chunk),
        grid=(m_total // tm,),
        in_specs=[
            pl.BlockSpec((tm, ka), lambda m: (m, 0)),
            pl.BlockSpec((tm, d), lambda m: (m, 0)),
            _resident((ka, d)),
            _resident((1, d)),
            _resident((1, d)),
            _resident((1, d)),
            _resident((d, dff)),
            _resident((dff, d)),
        ],
        out_specs=pl.BlockSpec((tm, d), lambda m: (m, 0)),
        out_shape=jax.ShapeDtypeStruct((m_total, d), F32),
        scratch_shapes=[pltpu.VMEM((tm, d), BF16), pltpu.VMEM((tm, dff), BF16)],
        compiler_params=pltpu.CompilerParams(
            dimension_semantics=("arbitrary",), vmem_limit_bytes=VMEM_LIMIT),
        name=f"outproj_mlp_k{ka}",
    )(a2d, h2d, w_out, g_post, g_mpre, g_mpost, w_up, w_down)


def _ret_tables(seq):
    freq = 1.0 / (RET_THETA ** jnp.linspace(0.0, 1.0, RET_DK // 2, dtype=F32))
    ang = jnp.arange(seq)[:, None].astype(F32) * freq[None, :]
    return jnp.cos(ang), jnp.sin(ang)


def _moba_tables(seq):
    half = ROPE_DIM // 2
    inv = ROPE_THETA ** (-jnp.arange(0, ROPE_DIM, 2, dtype=F32) / ROPE_DIM)
    ang = jnp.arange(seq)[:, None].astype(F32) * inv[None, :]
    cos, sin = jnp.cos(ang), jnp.sin(ang)
    gap = LANES // 2 - half
    c = jnp.concatenate([cos, jnp.ones((seq, gap), F32), cos, jnp.ones((seq, gap), F32)], axis=1)
    s = jnp.concatenate([-sin, jnp.zeros((seq, gap), F32), sin, jnp.zeros((seq, gap), F32)], axis=1)
    return c, s


def kernel(x, ln_mix_pre_0, ln_mix_post_0, w_in_0, w_out_0, ln_mlp_pre_0, ln_mlp_post_0, w_up_0, w_down_0,
           ln_mix_pre_1, ln_mix_post_1, w_in_1, w_out_1, ln_mlp_pre_1, ln_mlp_post_1, w_up_1, w_down_1):
    batch, seq, d = x.shape
    row = lambda g: g.reshape(1, d).astype(F32)

    log_gamma = jnp.log(1.0 - 2.0 ** (-5.0 - jnp.arange(RET_HEADS, dtype=F32)))

    h = x.reshape(batch * seq, d)

    (qk, vg), (w_out_0b, w_up_0b, w_down_0b, w_in_1b) = _inproj(
        h, row(ln_mix_pre_0), w_in_0.astype(BF16), _ret_tables(seq), mode="ret", batch=batch, seq=seq,
        cast=(w_out_0, w_up_0, w_down_0, w_in_1))
    mix = _retention(qk, vg, log_gamma, batch, seq)
    h = _outproj_mlp(mix.reshape(batch * seq, -1), h, w_out_0b, row(ln_mix_post_0),
                     row(ln_mlp_pre_0), row(ln_mlp_post_0), w_up_0b, w_down_0b)

    (qkv,), (w_out_1b, w_up_1b, w_down_1b) = _inproj(
        h, row(ln_mix_pre_1), w_in_1b, _moba_tables(seq), mode="moba", batch=batch, seq=seq,
        cast=(w_out_1, w_up_1, w_down_1))
    mix = _moba(qkv, batch, seq)
    h = _outproj_mlp(mix.reshape(batch * seq, -1), h, w_out_1b, row(ln_mix_post_1),
                     row(ln_mlp_pre_1), row(ln_mlp_post_1), w_up_1b, w_down_1b)
    return h.reshape(batch, seq, d)
```

```python
import functools
import math

import jax
import jax.numpy as jnp
from jax import lax
from jax.experimental import pallas as pl
from jax.experimental.pallas import tpu as pltpu

NORM_EPS = 1e-6
NEG = -1e30
LOG2E = math.log2(math.e)

RET_HEADS = 4
RET_DK = 256
RET_DV = 512
RET_THETA = 10000.0
RET_CHUNK = 256

MOBA_HEADS = 8
MOBA_DH = 128
MOBA_BLOCK = 256
MOBA_TOPK = 3
ROPE_THETA = 500000.0
ROPE_DIM = 32

LANES = 128
BF16_SUBLANES = 16
VMEM_LIMIT = 60 * 1024 * 1024

F32 = jnp.float32
BF16 = jnp.bfloat16
NT_DIMS = (((1,), (1,)), ((), ()))


def _rms_norm(x, g):
    return x * lax.rsqrt(jnp.mean(x * x, axis=-1, keepdims=True) + NORM_EPS) * g


def _resident(shape):
    return pl.BlockSpec(shape, lambda *_: (0,) * len(shape), pipeline_mode=pl.Buffered(1))


def _inproj_outputs(mode):
    if mode == "ret":
        return [(2 * RET_HEADS, RET_DK), (2 * RET_HEADS, RET_DV)]
    return [(3 * MOBA_HEADS, MOBA_DH)]


def _inproj_plan(mode):
    if mode == "ret":
        hq = RET_HEADS * RET_DK
        hv = RET_HEADS * RET_DV
        plan = [(1, hv + c * RET_DV, RET_DV, "swish", 1, RET_HEADS + c) for c in range(RET_HEADS)]
        plan += [(0, h * RET_DK, RET_DK, 1.0, 0, h) for h in range(RET_HEADS)]
        plan += [(0, hq + h * RET_DK, RET_DK, RET_DK ** -0.5, 0, RET_HEADS + h) for h in range(RET_HEADS)]
        plan += [(1, c * RET_DV, RET_DV, None, 1, c) for c in range(RET_HEADS)]
    else:
        hq = MOBA_HEADS * MOBA_DH
        per = 4
        cw = per * MOBA_DH
        plan = [(0, c * cw, cw, MOBA_DH ** -0.5 * LOG2E, 0, c * per) for c in range(hq // cw)]
        plan += [(0, hq + c * cw, cw, 1.0, 0, MOBA_HEADS + c * per) for c in range(hq // cw)]
        plan += [(1, c * cw, cw, None, 0, 2 * MOBA_HEADS + c * per) for c in range(hq // cw)]
    return plan


PERM_W = 256
NORM_PIECE = 256


def _qk_column_permutation(mode):
    new = lax.broadcasted_iota(jnp.int32, (PERM_W, PERM_W), 1)
    old = lax.broadcasted_iota(jnp.int32, (PERM_W, PERM_W), 0)
    if mode == "ret":
        half = RET_DK // 2
        src = jnp.where(new < half, 2 * new, 2 * (new - half) + 1)
    else:
        j = new % MOBA_DH
        h16, mid = ROPE_DIM // 2, MOBA_DH // 2
        src_j = jnp.where(j < h16, j, jnp.where(j < mid, j + h16, jnp.where(j < mid + h16, j - (mid - h16), j)))
        src = (new - j) + src_j
    return (old == src).astype(BF16)


def _inproj_kernel(x_ref, g_ref, w_ref, cos_ref, sin_ref, *rest, mode, n_cast):
    cast_in, rest = rest[:n_cast], rest[n_cast:]
    n_out = len(rest) - n_cast - 2
    out_refs, cast_out, (xs_ref, wqk_ref) = rest[:n_out], rest[n_out:n_out + n_cast], rest[n_out + n_cast:]
    n_qk = wqk_ref.shape[1]
    for src, dst in zip(cast_in, cast_out):
        dst[...] = src[...].astype(dst.dtype)

    @pl.when(pl.program_id(0) == 0)
    def _():
        perm = _qk_column_permutation(mode)
        for c0 in range(0, n_qk, PERM_W):
            cols = slice(c0, c0 + PERM_W)
            wqk_ref[:, cols] = jnp.dot(w_ref[:, cols], perm, preferred_element_type=F32).astype(BF16)

    tm = x_ref.shape[0]
    pieces = [pl.ds(r, NORM_PIECE) for r in range(0, tm, NORM_PIECE)]
    for rows in pieces:
        xs_ref[rows, :] = _rms_norm(x_ref[rows, :], g_ref[...]).astype(BF16)
    scaled = {}
    for idx, (wi, start, width, scale, oi, slab0) in enumerate(_inproj_plan(mode)):
        o_ref = out_refs[oi]
        sw = o_ref.shape[-1]
        w_cols = wqk_ref[:, start:start + width] if wi == 0 else w_ref[:, n_qk + start:n_qk + start + width]
        for rows in (pieces if idx == 0 else [pl.ds(0, tm)]):
            acc = jnp.dot(xs_ref[rows, :], w_cols, preferred_element_type=F32)
            if scale is None or scale == "swish":
                if scale == "swish":
                    half = 0.5 * acc
                    acc = half + half * jnp.tanh(half)
                for i in range(width // sw):
                    o_ref[0, slab0 + i, rows, :] = acc[:, i * sw:(i + 1) * sw].astype(o_ref.dtype)
                continue
            if mode == "ret":
                head, is_k = slab0 % RET_HEADS, slab0 >= RET_HEADS
                log_gamma = math.log(1.0 - 2.0 ** (-5.0 - head))
                if ("pos", rows.start, rows.size) not in scaled:
                    pos = lax.broadcasted_iota(jnp.int32, (rows.size, LANES), 0) + rows.start
                    scaled[("pos", rows.start, rows.size)] = jnp.bitwise_and(pos, RET_CHUNK - 1).astype(F32)
                in_chunk = scaled[("pos", rows.start, rows.size)]
                decay = jnp.exp(in_chunk * (-log_gamma if is_k else log_gamma) + math.log(scale))
                c, s = cos_ref[rows, :] * decay, sin_ref[rows, :] * decay
                x1, x2 = acc[:, :LANES], acc[:, LANES:]
                o_ref[0, slab0, rows, 0:LANES] = (x1 * c - x2 * s).astype(o_ref.dtype)
                o_ref[0, slab0, rows, LANES:width] = (x1 * s + x2 * c).astype(o_ref.dtype)
            else:
                if (scale, rows.start, rows.size) not in scaled:
                    scaled[(scale, rows.start, rows.size)] = (cos_ref[rows, :] * scale, sin_ref[rows, :] * scale)
                c, s = scaled[(scale, rows.start, rows.size)]
                for i in range(width // sw):
                    xh = acc[:, i * sw:(i + 1) * sw]
                    y = xh * c + pltpu.roll(xh, LANES // 2, 1) * s
                    o_ref[0, slab0 + i, rows, :] = y.astype(o_ref.dtype)


def _inproj(x2d, gain, w, tabs, *, mode, batch, seq, cast=(), tm=1024):
    m_total, d = x2d.shape
    n_qk = 2 * (RET_HEADS * RET_DK if mode == "ret" else MOBA_HEADS * MOBA_DH)
    tiles_per_seq = seq // tm
    steps = m_total // tm
    tab_spec = pl.BlockSpec((tm, LANES), lambda m: (m % tiles_per_seq, 0))
    outs = _inproj_outputs(mode)
    for a in cast:
        assert a.shape[0] % (steps * BF16_SUBLANES) == 0, a.shape
    cast_specs = [pl.BlockSpec((a.shape[0] // steps, a.shape[1]), lambda m: (m, 0)) for a in cast]
    res = pl.pallas_call(
        functools.partial(_inproj_kernel, mode=mode, n_cast=len(cast)),
        grid=(steps,),
        in_specs=[pl.BlockSpec((tm, d), lambda m: (m, 0)), _resident((1, d)), _resident(w.shape),
                  tab_spec, tab_spec, *cast_specs],
        out_specs=[pl.BlockSpec((1, ns, tm, sw), lambda m: (m // tiles_per_seq, 0, m % tiles_per_seq, 0))
                   for ns, sw in outs] + cast_specs,
        out_shape=[jax.ShapeDtypeStruct((batch, ns, seq, sw), BF16) for ns, sw in outs]
        + [jax.ShapeDtypeStruct(a.shape, BF16) for a in cast],
        scratch_shapes=[pltpu.VMEM((tm, d), BF16), pltpu.VMEM((d, n_qk), BF16)],
        compiler_params=pltpu.CompilerParams(
            dimension_semantics=("arbitrary",), vmem_limit_bytes=VMEM_LIMIT),
        name=f"inproj_{mode}",
    )(x2d, gain, w, *tabs, *cast)
    return res[:len(outs)], res[len(outs):]


RET_HEADS_PER_STEP = 2


def _retention_kernel(lg_ref, q_ref, k_ref, v_ref, g_ref, o_ref, state_ref):
    L = RET_CHUNK
    hp, seq, _ = q_ref.shape
    dv = v_ref.shape[2]
    head0 = pl.program_id(1) * hp
    chunk_decay = [jnp.exp(jnp.full((1, 1), lg_ref[head0 + i] * L, F32)) for i in range(hp)]
    causal = lax.broadcasted_iota(jnp.int32, (L, L), 0) >= lax.broadcasted_iota(jnp.int32, (L, L), 1)

    state_ref[...] = jnp.zeros_like(state_ref)
    for c in range(seq // L):
        rows = pl.ds(c * L, L)
        for i in range(hp):
            qc = q_ref[i, rows, :]
            kc = k_ref[i, rows, :]
            vc = v_ref[i, rows, :]
            scores = jnp.where(causal, lax.dot_general(qc, kc, NT_DIMS, preferred_element_type=F32), 0.0)
            state = state_ref[i]
            o = (jnp.dot(scores.astype(BF16), vc, preferred_element_type=F32)
                 + jnp.dot(qc, state.astype(BF16), preferred_element_type=F32))
            k_t = kc.astype(F32).T.astype(BF16)
            state_ref[i] = (state + jnp.dot(k_t, vc, preferred_element_type=F32)) * chunk_decay[i]
            o = o * lax.rsqrt(jnp.mean(o * o, axis=-1, keepdims=True) + NORM_EPS)
            o_ref[rows, i * dv:(i + 1) * dv] = (o * g_ref[i, rows, :].astype(F32)).astype(o_ref.dtype)


def _retention(qk, vg, log_gamma, batch, seq):
    H, DK, DV, HP = RET_HEADS, RET_DK, RET_DV, RET_HEADS_PER_STEP
    groups = H // HP
    grid_spec = pltpu.PrefetchScalarGridSpec(
        num_scalar_prefetch=1,
        grid=(batch, groups),
        in_specs=[
            pl.BlockSpec((None, HP, seq, DK), lambda b, g, lg: (b, g, 0, 0)),
            pl.BlockSpec((None, HP, seq, DK), lambda b, g, lg: (b, groups + g, 0, 0)),
            pl.BlockSpec((None, HP, seq, DV), lambda b, g, lg: (b, g, 0, 0)),
            pl.BlockSpec((None, HP, seq, DV), lambda b, g, lg: (b, groups + g, 0, 0)),
        ],
        out_specs=pl.BlockSpec((None, seq, HP * DV), lambda b, g, lg: (b, 0, g)),
        scratch_shapes=[pltpu.VMEM((HP, DK, DV), F32)],
    )
    return pl.pallas_call(
        _retention_kernel,
        grid_spec=grid_spec,
        out_shape=jax.ShapeDtypeStruct((batch, seq, H * DV), BF16),
        compiler_params=pltpu.CompilerParams(
            dimension_semantics=("arbitrary", "arbitrary"), vmem_limit_bytes=VMEM_LIMIT),
        name="retention",
    )(log_gamma, qk, qk, vg, vg)


MOBA_HEADS_PER_STEP = 2
MOBA_SCORE_BUFFERS = 2
MOBA_SCRATCH_PER_HEAD = 3 + 2 * MOBA_SCORE_BUFFERS


def _moba_kernel(q_ref, k_ref, v_ref, o_ref, *scratch):
    dh = q_ref.shape[2]
    heads = []
    for i in range(q_ref.shape[0]):
        bufs = scratch[i * MOBA_SCRATCH_PER_HEAD:(i + 1) * MOBA_SCRATCH_PER_HEAD]
        heads.append(_moba_head(q_ref.at[i], k_ref.at[i], v_ref.at[i], o_ref, i * dh, *bufs))
    done = object()
    while heads:
        heads = [g for g in heads if next(g, done) is not done]


def _moba_head(q_ref, k_ref, v_ref, o_ref, col0, ka_ref, ksum_ref, vt_ref, *sp_refs):
    nbuf = len(sp_refs) // 2
    s_bufs, p_bufs = sp_refs[:nbuf], sp_refs[nbuf:]
    BS = MOBA_BLOCK
    seq, dh = q_ref.shape
    nb = seq // BS
    G = ka_ref.shape[0] - seq

    for j in range(nb):
        ksum = jnp.sum(k_ref[pl.ds(j * BS, BS), :].astype(F32), axis=0, keepdims=True)
        hi = ksum.astype(BF16).astype(F32)
        ksum_ref[j:j + 1, :] = hi
        ksum_ref[nb + j:nb + j + 1, :] = ksum - hi
    ka_ref[0:G, :] = ksum_ref[...].astype(BF16)
    ka_ref[G:, :] = k_ref[...]
    gate_floor = NEG * BS * LOG2E

    QT = 2
    order = list(range(nb // QT - 1, -1, -1))

    def scores(i):
        t = order[i]
        nr = G + (t + 1) * QT * BS
        s_bufs[i % nbuf][0:nr, :] = lax.dot_general(ka_ref[0:nr, :], q_ref[pl.ds(t * QT * BS, QT * BS), :],
                                                 NT_DIMS, preferred_element_type=F32)

    scores(0)

    for j in range(nb):
        cols = pl.ds(j * BS, BS)
        vt_ref[0:dh, cols] = v_ref[cols, :].astype(F32).T.astype(BF16)
    vt_ref[dh:, :] = jnp.ones((vt_ref.shape[0] - dh, seq), BF16)

    key_idx = lax.broadcasted_iota(jnp.int32, (BS, BS), 0)
    qry_idx = lax.broadcasted_iota(jnp.int32, (BS, BS), 1)
    causal = key_idx <= qry_idx
    yield

    for i, t in enumerate(order):
        nk = (t + 1) * QT * BS
        if nbuf > 1 and i + 1 < len(order):
            scores(i + 1)
        if nbuf == 1 and i > 0:
            scores(i)
        s_cur, p_cur = s_bufs[i % nbuf], p_bufs[i % nbuf]

        for hf in range(QT):
            n = t * QT + hf
            cs = pl.ds(hf * BS, BS)

            gate = [s_cur[j:j + 1, cs] + s_cur[nb + j:nb + j + 1, cs] for j in range(n)]

            sels = []
            for j in range(n):
                rank = jnp.where(gate[j] < gate_floor, float(nb - n), 0.0)
                for r in range(n):
                    if r != j:
                        ahead = (gate[r] >= gate[j]) if r < j else (gate[r] > gate[j])
                        rank = rank + ahead.astype(F32)
                sels.append(rank < MOBA_TOPK)

            s_own = jnp.where(causal, s_cur[pl.ds(G + n * BS, BS), cs], NEG)
            m = jnp.max(s_own, axis=0, keepdims=True)
            for j in range(n):
                m_blk = jnp.max(s_cur[pl.ds(G + j * BS, BS), cs], axis=0, keepdims=True)
                m = jnp.maximum(m, jnp.where(sels[j], m_blk, NEG))

            p_cur[pl.ds(n * BS, BS), cs] = jnp.exp2(s_own - m).astype(BF16)
            for j in range(n):
                p = jnp.exp2(s_cur[pl.ds(G + j * BS, BS), cs] - jnp.where(sels[j], m, -NEG))
                p_cur[pl.ds(j * BS, BS), cs] = p.astype(BF16)
            for j in range(n + 1, (t + 1) * QT):
                p_cur[pl.ds(j * BS, BS), cs] = jnp.zeros((BS, BS), BF16)

        acc = jnp.dot(vt_ref[:, 0:nk], p_cur[0:nk, :], preferred_element_type=F32)
        for hf in range(QT):
            cs = slice(hf * BS, (hf + 1) * BS)
            out = acc[0:dh, cs] * (1.0 / acc[dh:dh + 1, cs])
            o_ref[pl.ds((t * QT + hf) * BS, BS), col0:col0 + dh] = out.T.astype(o_ref.dtype)
        yield


def _moba(qkv, batch, seq):
    H, DH, HP = MOBA_HEADS, MOBA_DH, MOBA_HEADS_PER_STEP
    assert 2 * (seq // MOBA_BLOCK) == BF16_SUBLANES, "hi and lo key-sum rows must fill one bf16 tile"
    groups = H // HP
    per_head = [pltpu.VMEM((BF16_SUBLANES + seq, DH), BF16),
                pltpu.VMEM((BF16_SUBLANES, DH), F32),
                pltpu.VMEM((DH + BF16_SUBLANES, seq), BF16),
                *[pltpu.VMEM((BF16_SUBLANES + seq, 2 * MOBA_BLOCK), F32)] * MOBA_SCORE_BUFFERS,
                *[pltpu.VMEM((seq, 2 * MOBA_BLOCK), BF16)] * MOBA_SCORE_BUFFERS]
    assert len(per_head) == MOBA_SCRATCH_PER_HEAD
    return pl.pallas_call(
        _moba_kernel,
        grid=(batch, groups),
        in_specs=[
            pl.BlockSpec((None, HP, seq, DH), lambda b, g: (b, g, 0, 0)),
            pl.BlockSpec((None, HP, seq, DH), lambda b, g: (b, groups + g, 0, 0)),
            pl.BlockSpec((None, HP, seq, DH), lambda b, g: (b, 2 * groups + g, 0, 0)),
        ],
        out_specs=pl.BlockSpec((None, seq, HP * DH), lambda b, g: (b, 0, g)),
        out_shape=jax.ShapeDtypeStruct((batch, seq, H * DH), BF16),
        scratch_shapes=per_head * HP,
        compiler_params=pltpu.CompilerParams(
            dimension_semantics=("arbitrary", "arbitrary"), vmem_limit_bytes=VMEM_LIMIT),
        name="moba",
    )(qkv, qkv, qkv)


def _outproj_mlp_kernel(a_ref, h_ref, wo_ref, gpost_ref, gpre_ref, gmpost_ref, wup_ref, wdn_ref,
                        o_ref, xn_ref, hid_ref, *, sub, ff_chunk):
    assert sum(sub) == a_ref.shape[0]
    subs = [pl.ds(sum(sub[:i]), sub[i]) for i in range(len(sub))]
    for rows in subs:
        mix = jnp.dot(a_ref[rows, :], wo_ref[...], preferred_element_type=F32)
        h1 = h_ref[rows, :] + _rms_norm(mix, gpost_ref[...])
        o_ref[rows, :] = h1
        xn_ref[rows, :] = _rms_norm(h1, gpre_ref[...]).astype(BF16)
    for rows in subs:
        for j in range(wup_ref.shape[1] // ff_chunk):
            cols = pl.ds(j * ff_chunk, ff_chunk)
            u = jnp.maximum(jnp.dot(xn_ref[rows, :], wup_ref[:, cols], preferred_element_type=F32), 0.0)
            hid_ref[rows, cols] = (u * u).astype(BF16)
    for rows in subs:
        dn = jnp.dot(hid_ref[rows, :], wdn_ref[...], preferred_element_type=F32)
        o_ref[rows, :] = o_ref[rows, :] + _rms_norm(dn, gmpost_ref[...])


def _outproj_mlp(a2d, h2d, w_out, g_post, g_mpre, g_mpost, w_up, w_down, *, tm=1024, sub=(256, 256, 256, 256), ff_chunk=1024):
    m_total, d = h2d.shape
    ka = a2d.shape[1]
    dff = w_up.shape[1]
    return pl.pallas_call(
        functools.partial(_outproj_mlp_kernel, sub=sub, ff_chunk=ff_chunk),
        grid=(m_total // tm,),
        in_specs=[
            pl.BlockSpec((tm, ka), lambda m: (m, 0)),
            pl.BlockSpec((tm, d), lambda m: (m, 0)),
            _resident((ka, d)),
            _resident((1, d)),
            _resident((1, d)),
            _resident((1, d)),
            _resident((d, dff)),
            _resident((dff, d)),
        ],
        out_specs=pl.BlockSpec((tm, d), lambda m: (m, 0)),
        out_shape=jax.ShapeDtypeStruct((m_total, d), F32),
        scratch_shapes=[pltpu.VMEM((tm, d), BF16), pltpu.VMEM((tm, dff), BF16)],
        compiler_params=pltpu.CompilerParams(
            dimension_semantics=("arbitrary",), vmem_limit_bytes=VMEM_LIMIT),
        name=f"outproj_mlp_k{ka}",
    )(a2d, h2d, w_out, g_post, g_mpre, g_mpost, w_up, w_down)


def _ret_tables(seq):
    freq = 1.0 / (RET_THETA ** jnp.linspace(0.0, 1.0, RET_DK // 2, dtype=F32))
    ang = jnp.arange(seq)[:, None].astype(F32) * freq[None, :]
    return jnp.cos(ang), jnp.sin(ang)


def _moba_tables(seq):
    half = ROPE_DIM // 2
    inv = ROPE_THETA ** (-jnp.arange(0, ROPE_DIM, 2, dtype=F32) / ROPE_DIM)
    ang = jnp.arange(seq)[:, None].astype(F32) * inv[None, :]
    cos, sin = jnp.cos(ang), jnp.sin(ang)
    gap = LANES // 2 - half
    c = jnp.concatenate([cos, jnp.ones((seq, gap), F32), cos, jnp.ones((seq, gap), F32)], axis=1)
    s = jnp.concatenate([-sin, jnp.zeros((seq, gap), F32), sin, jnp.zeros((seq, gap), F32)], axis=1)
    return c, s


def kernel(x, ln_mix_pre_0, ln_mix_post_0, w_in_0, w_out_0, ln_mlp_pre_0, ln_mlp_post_0, w_up_0, w_down_0,
           ln_mix_pre_1, ln_mix_post_1, w_in_1, w_out_1, ln_mlp_pre_1, ln_mlp_post_1, w_up_1, w_down_1):
    batch, seq, d = x.shape
    row = lambda g: g.reshape(1, d).astype(F32)

    log_gamma = jnp.log(1.0 - 2.0 ** (-5.0 - jnp.arange(RET_HEADS, dtype=F32)))

    h = x.reshape(batch * seq, d)

    (qk, vg), (w_out_0b, w_up_0b, w_down_0b, w_in_1b) = _inproj(
        h, row(ln_mix_pre_0), w_in_0.astype(BF16), _ret_tables(seq), mode="ret", batch=batch, seq=seq,
        cast=(w_out_0, w_up_0, w_down_0, w_in_1))
    mix = _retention(qk, vg, log_gamma, batch, seq)
    h = _outproj_mlp(mix.reshape(batch * seq, -1), h, w_out_0b, row(ln_mix_post_0),
                     row(ln_mlp_pre_0), row(ln_mlp_post_0), w_up_0b, w_down_0b)

    (qkv,), (w_out_1b, w_up_1b, w_down_1b) = _inproj(
        h, row(ln_mix_pre_1), w_in_1b, _moba_tables(seq), mode="moba", batch=batch, seq=seq,
        cast=(w_out_1, w_up_1, w_down_1))
    mix = _moba(qkv, batch, seq)
    h = _outproj_mlp(mix.reshape(batch * seq, -1), h, w_out_1b, row(ln_mix_post_1),
                     row(ln_mlp_pre_1), row(ln_mlp_post_1), w_up_1b, w_down_1b)
    return h.reshape(batch, seq, d)
```
